```python
import jax, jax.numpy as jnp
from jax import lax
import numpy as np


D_MODEL = 1024
BATCH = 8
SEQ = 4096
DEPTH = 4

N_MIXERS = 3
POOL_WINDOWS = (2, 4, 8, 16)
POOL_GROUPS = len(POOL_WINDOWS)
POOL_GROUP_DIM = D_MODEL // POOL_GROUPS
SGU_CHUNK = 128
SGU_WIDTH = D_MODEL
SGU_HEAD_DIM = 128
SGU_HEADS = SGU_WIDTH // SGU_HEAD_DIM
MLA_HEADS = 16
MLA_Q_LORA = 256
MLA_KV_LORA = 128
MLA_NOPE = 128
MLA_ROPE = 64
MLA_V = 128
ROPE_THETA = 10000.0
Q_BLOCK = 128
D_FF = 4 * D_MODEL
RMS_EPS = 1e-6
LN_EPS = 1e-5
MAX_POS_OFFSET = 4096
N_POOL_LAYERS = len(range(0, DEPTH, N_MIXERS))
N_SGU_LAYERS = len(range(1, DEPTH, N_MIXERS))
N_MLA_LAYERS = len(range(2, DEPTH, N_MIXERS))

kernel_name = 'hybrid_pool_sgu_mla_decoder'


def rmsnorm(x, g):
    xf = x.astype(jnp.float32)
    y = xf * lax.rsqrt(jnp.mean(xf * xf, axis=-1, keepdims=True) + RMS_EPS)
    return (y * g.astype(jnp.float32)).astype(x.dtype)


def layernorm(x, g, b):
    xf = x.astype(jnp.float32)
    mu = jnp.mean(xf, axis=-1, keepdims=True)
    xc = xf - mu
    var = jnp.mean(xc * xc, axis=-1, keepdims=True)
    y = xc * lax.rsqrt(var + LN_EPS)
    return (y * g.astype(jnp.float32) + b.astype(jnp.float32)).astype(x.dtype)


def modulate(h, shift, scale):
    return h * (1.0 + scale[:, None, :]) + shift[:, None, :]


def pool_mixer(h, w, scale):
    B_, S_, _ = h.shape
    hf = h.astype(jnp.float32).reshape(B_, S_, POOL_GROUPS, POOL_GROUP_DIM)
    cs = jnp.cumsum(hf, axis=1)
    cs = jnp.concatenate([jnp.zeros_like(cs[:, :1]), cs], axis=1)
    t = jnp.arange(S_, dtype=jnp.float32)
    outs = []
    for gi, win in enumerate(POOL_WINDOWS):
        csg = cs[:, :, gi]
        upper = csg[:, 1:]
        lower = jnp.concatenate([jnp.zeros_like(csg[:, :win - 1]), csg[:, :S_ - win + 1]], axis=1)
        count = jnp.minimum(t + 1.0, float(win))[None, :, None]
        outs.append((upper - lower) / count - hf[:, :, gi])
    pooled = jnp.stack(outs, axis=2).astype(h.dtype)
    y = jnp.einsum('bsgc,gcd->bsgd', pooled, w).reshape(B_, S_, D_MODEL)
    return y * scale


def sgu_mixer(h, w_in, ln_g, ln_b, w_s, b_s, w_out):
    B_, S_, _ = h.shape
    z = jax.nn.gelu(h @ w_in, approximate=False)
    u, v = jnp.split(z, 2, axis=-1)
    v = layernorm(v, ln_g, ln_b)
    nc = S_ // SGU_CHUNK
    v = v.reshape(B_, nc, SGU_CHUNK, SGU_HEADS, SGU_HEAD_DIM)
    mask = jnp.tril(jnp.ones((SGU_CHUNK, SGU_CHUNK), dtype=bool))
    ws = jnp.where(mask[None], w_s, 0)
    mixed = jnp.einsum('hts,bnshc->bnthc', ws, v) + b_s.T[None, None, :, :, None]
    gated = u * mixed.reshape(B_, S_, SGU_WIDTH)
    return gated @ w_out


def apply_rope(x, cos, sin):
    x1, x2 = jnp.split(x, 2, axis=-1)
    return jnp.concatenate([x1 * cos - x2 * sin, x2 * cos + x1 * sin], axis=-1)


def mla_mixer(h, positions, w_dq_dkv, q_norm_g, kv_norm_g, w_uq, w_ukv, w_o):
    B_, S_, _ = h.shape
    lat = h @ w_dq_dkv
    c_q, c_kv, k_rope = jnp.split(lat, [MLA_Q_LORA, MLA_Q_LORA + MLA_KV_LORA], axis=-1)
    c_q = rmsnorm(c_q, q_norm_g)
    c_kv = rmsnorm(c_kv, kv_norm_g)
    q = (c_q @ w_uq).reshape(B_, S_, MLA_HEADS, MLA_NOPE + MLA_ROPE)
    q_nope, q_rope = jnp.split(q, [MLA_NOPE], axis=-1)
    kv = (c_kv @ w_ukv).reshape(B_, S_, MLA_HEADS, MLA_NOPE + MLA_V)
    k_nope, v = jnp.split(kv, [MLA_NOPE], axis=-1)
    inv_freq = ROPE_THETA ** (-jnp.arange(0, MLA_ROPE, 2, dtype=jnp.float32) / MLA_ROPE)
    ang = positions.astype(jnp.float32)[..., None] * inv_freq
    cos, sin = jnp.cos(ang), jnp.sin(ang)
    q_rope = apply_rope(q_rope.astype(jnp.float32), cos[:, :, None], sin[:, :, None]).astype(h.dtype)
    k_rope = apply_rope(k_rope.astype(jnp.float32), cos, sin).astype(h.dtype)
    sm_scale = (MLA_NOPE + MLA_ROPE) ** -0.5
    nb = S_ // Q_BLOCK

    def to_blocks(t):
        return jnp.moveaxis(t.reshape(B_, nb, Q_BLOCK, *t.shape[2:]), 1, 0)

    k_idx = jnp.arange(S_)

    def attend_block(args):
        qn, qr, blk = args
        s = jnp.einsum('bqhd,bkhd->bhqk', qn, k_nope, preferred_element_type=jnp.float32)
        s = s + jnp.einsum('bqhr,bkr->bhqk', qr, k_rope, preferred_element_type=jnp.float32)
        q_idx = blk * Q_BLOCK + jnp.arange(Q_BLOCK)
        causal = k_idx[None, :] <= q_idx[:, None]
        s = jnp.where(causal[None, None], s * sm_scale, -1e30)
        p = jax.nn.softmax(s, axis=-1).astype(v.dtype)
        return jnp.einsum('bhqk,bkhd->bqhd', p, v)

    out = lax.map(attend_block, (to_blocks(q_nope), to_blocks(q_rope), jnp.arange(nb)))
    out = jnp.moveaxis(out, 0, 1).reshape(B_, S_, MLA_HEADS * MLA_V)
    return out @ w_o


def sq_relu_mlp(h, w1, w2):
    return jnp.square(jax.nn.relu(h @ w1)) @ w2


def setup_inputs(seed: int = 0) -> dict:
    key = jax.random.key(seed)
    ks = jax.random.split(key, 32)
    f32 = jnp.float32

    def nrm(k, shape, std):
        return jax.random.normal(k, shape, f32) * std

    def gain(k, shape):
        return 1.0 + 0.1 * jax.random.normal(k, shape, f32)

    x = jax.random.normal(ks[0], (BATCH, SEQ, D_MODEL), f32)
    c = jax.random.normal(ks[1], (BATCH, D_MODEL), f32)
    offset = jax.random.randint(ks[2], (BATCH, 1), 0, MAX_POS_OFFSET, dtype=jnp.int32)
    positions = (offset + jnp.arange(SEQ, dtype=jnp.int32)[None, :]).astype(jnp.int32)
    return {
        'x': x,
        'c': c,
        'positions': positions,
        'ada_w': nrm(ks[3], (DEPTH, D_MODEL, 6 * D_MODEL), 0.5 * D_MODEL ** -0.5),
        'ada_b': nrm(ks[4], (DEPTH, 6 * D_MODEL), 0.02),
        'norm_mix_g': gain(ks[5], (DEPTH, D_MODEL)),
        'norm_mlp_g': gain(ks[6], (DEPTH, D_MODEL)),
        'pool_w': nrm(ks[7], (N_POOL_LAYERS, POOL_GROUPS, POOL_GROUP_DIM, POOL_GROUP_DIM), POOL_GROUP_DIM ** -0.5),
        'pool_scale': gain(ks[8], (N_POOL_LAYERS, D_MODEL)),
        'sgu_w_in': nrm(ks[9], (N_SGU_LAYERS, D_MODEL, 2 * SGU_WIDTH), D_MODEL ** -0.5),
        'sgu_ln_g': gain(ks[10], (N_SGU_LAYERS, SGU_WIDTH)),
        'sgu_ln_b': nrm(ks[11], (N_SGU_LAYERS, SGU_WIDTH), 0.02),
        'sgu_w_s': nrm(ks[12], (N_SGU_LAYERS, SGU_HEADS, SGU_CHUNK, SGU_CHUNK), SGU_CHUNK ** -0.5),
        'sgu_b_s': gain(ks[13], (N_SGU_LAYERS, SGU_HEADS, SGU_CHUNK)),
        'sgu_w_out': nrm(ks[14], (N_SGU_LAYERS, SGU_WIDTH, D_MODEL), SGU_WIDTH ** -0.5),
        'mla_w_dq_dkv': nrm(ks[15], (N_MLA_LAYERS, D_MODEL, MLA_Q_LORA + MLA_KV_LORA + MLA_ROPE), D_MODEL ** -0.5),
        'mla_q_norm_g': gain(ks[16], (N_MLA_LAYERS, MLA_Q_LORA)),
        'mla_kv_norm_g': gain(ks[17], (N_MLA_LAYERS, MLA_KV_LORA)),
        'mla_w_uq': nrm(ks[18], (N_MLA_LAYERS, MLA_Q_LORA, MLA_HEADS * (MLA_NOPE + MLA_ROPE)), MLA_Q_LORA ** -0.5),
        'mla_w_ukv': nrm(ks[19], (N_MLA_LAYERS, MLA_KV_LORA, MLA_HEADS * (MLA_NOPE + MLA_V)), MLA_KV_LORA ** -0.5),
        'mla_w_o': nrm(ks[20], (N_MLA_LAYERS, MLA_HEADS * MLA_V, D_MODEL), (MLA_HEADS * MLA_V) ** -0.5),
        'mlp_w1': nrm(ks[21], (DEPTH, D_MODEL, D_FF), D_MODEL ** -0.5),
        'mlp_w2': nrm(ks[22], (DEPTH, D_FF, D_MODEL), D_FF ** -0.5),
        'final_g': gain(ks[23], (D_MODEL,)),
    }


def reference(x, c, positions, ada_w, ada_b, norm_mix_g, norm_mlp_g, pool_w, pool_scale,
              sgu_w_in, sgu_ln_g, sgu_ln_b, sgu_w_s, sgu_b_s, sgu_w_out,
              mla_w_dq_dkv, mla_q_norm_g, mla_kv_norm_g, mla_w_uq, mla_w_ukv, mla_w_o,
              mlp_w1, mlp_w2, final_g):
    c_act = jax.nn.silu(c)
    for i in range(DEPTH):
        mod = c_act @ ada_w[i] + ada_b[i]
        sh1, sc1, g1, sh2, sc2, g2 = jnp.split(mod, 6, axis=-1)
        h = modulate(rmsnorm(x, norm_mix_g[i]), sh1, sc1)
        kind, j = i % N_MIXERS, i // N_MIXERS
        if kind == 0:
            y = pool_mixer(h, pool_w[j], pool_scale[j])
        elif kind == 1:
            y = sgu_mixer(h, sgu_w_in[j], sgu_ln_g[j], sgu_ln_b[j], sgu_w_s[j], sgu_b_s[j], sgu_w_out[j])
        else:
            y = mla_mixer(h, positions, mla_w_dq_dkv[j], mla_q_norm_g[j], mla_kv_norm_g[j],
                          mla_w_uq[j], mla_w_ukv[j], mla_w_o[j])
        x = x + g1[:, None, :] * y
        h = modulate(rmsnorm(x, norm_mlp_g[i]), sh2, sc2)
        x = x + g2[:, None, :] * sq_relu_mlp(h, mlp_w1[i], mlp_w2[i])
    return rmsnorm(x, final_g)
```

```python
import functools

import jax
import jax.numpy as jnp
from jax import lax
from jax.experimental import pallas as pl
from jax.experimental.pallas import tpu as pltpu

F32 = jnp.float32
BF16 = jnp.bfloat16

POOL_WINDOWS = (2, 4, 8, 16)
SGU_CHUNK = 128
SGU_HEAD_DIM = 128
MLA_HEADS = 16
MLA_Q_LORA = 256
MLA_KV_LORA = 128
MLA_NOPE = 128
MLA_ROPE = 64
MLA_V = 128
ROPE_THETA = 10000.0
RMS_EPS = 1e-6
LN_EPS = 1e-5
N_MIXERS = 3
MASK_VALUE = -1e30

LANES = 128
SUBLANES = 8
VMEM_LIMIT_BYTES = 56 * 1024 * 1024

TOKEN_TILE = 512
POOL_HALO = 16
ADA_COL_TILE = 2048
MLP_FF_TILE = 1024
ATTN_TILE = 256
ATTN_HEAD_GROUP = 4
QK_WIDTH = 2 * LANES


def _params(*semantics):
  return pltpu.CompilerParams(dimension_semantics=semantics,
                              vmem_limit_bytes=VMEM_LIMIT_BYTES)


def _const_spec(shape):
  zeros = (0,) * len(shape)
  return pl.BlockSpec(shape, lambda *_: zeros, pipeline_mode=pl.Buffered(1))


def _rms(x, eps):
  return x * lax.rsqrt(jnp.mean(x * x, axis=-1, keepdims=True) + eps)


def _norm_mod(x, g, shift, scale):
  return _rms(x, RMS_EPS) * g * (1.0 + scale) + shift


def _gelu(x):
  return 0.5 * x * (1.0 + lax.erf(x * (2.0 ** -0.5)))


def _ada_kernel(c_ref, w_ref, b_ref, o_ref):
  c = c_ref[...]
  ca = (c * jax.nn.sigmoid(c)).astype(BF16)
  o_ref[...] = jnp.dot(ca, w_ref[...].astype(BF16),
                       preferred_element_type=F32) + b_ref[...]


def _ada_mod(c, ada_w, ada_b):
  depth, d, d6 = ada_w.shape
  b = c.shape[0]
  tn = ADA_COL_TILE
  out = pl.pallas_call(
      _ada_kernel,
      grid=(depth, d6 // tn),
      in_specs=[
          pl.BlockSpec((b, d), lambda l, j: (0, 0)),
          pl.BlockSpec((None, d, tn), lambda l, j: (l, 0, j)),
          pl.BlockSpec((None, 1, tn), lambda l, j: (l, 0, j)),
      ],
      out_specs=pl.BlockSpec((None, b, tn), lambda l, j: (l, 0, j)),
      out_shape=jax.ShapeDtypeStruct((depth, b, d6), F32),
      compiler_params=_params("arbitrary", "arbitrary"),
      name="ada_mod",
  )(c, ada_w, ada_b.reshape(depth, 1, d6))
  return out.reshape(depth, b, 6, d)


def _pool_kernel(x_ref, xh_ref, mod_ref, g_ref, w_ref, ps_ref, o_ref, *, tm):
  i = pl.program_id(1)
  g = g_ref[...]
  shift, scale, gate = mod_ref[0:1, :], mod_ref[1:2, :], mod_ref[2:3, :]
  x = x_ref[...]
  h = _norm_mod(x, g, shift, scale)
  hh = _norm_mod(xh_ref[...], g, shift, scale)
  hh = jnp.where(i > 0, hh, 0.0)
  a = jnp.concatenate([hh, h], axis=0)
  t = (i * tm + lax.broadcasted_iota(jnp.int32, (tm, 1), 0)).astype(F32)
  gdim = w_ref.shape[1]
  for gi, win in enumerate(POOL_WINDOWS):
    cols = slice(gi * gdim, (gi + 1) * gdim)
    s = a[:, cols]
    k = 1
    while k < win:
      s = s + pltpu.roll(s, k, axis=0)
      k *= 2
    s = s[POOL_HALO:]
    inv_count = 1.0 / jnp.minimum(t + 1.0, float(win))
    pooled = s * inv_count - h[:, cols]
    y = jnp.dot(pooled.astype(BF16), w_ref[gi], preferred_element_type=F32)
    o_ref[:, cols] = x[:, cols] + gate[:, cols] * (y * ps_ref[:, cols])


def _pool_layer(x, mod, layer, g, w, ps):
  b, s, d = x.shape
  tm = TOKEN_TILE
  groups, gdim, _ = w.shape
  assert POOL_HALO >= max(POOL_WINDOWS) and tm % POOL_HALO == 0
  blocks_per_tile = tm // POOL_HALO
  return pl.pallas_call(
      functools.partial(_pool_kernel, tm=tm),
      grid=(b, s // tm),
      in_specs=[
          pl.BlockSpec((None, tm, d), lambda bi, i: (bi, i, 0)),
          pl.BlockSpec((None, POOL_HALO, d),
                       lambda bi, i: (bi, jnp.maximum(i * blocks_per_tile - 1, 0), 0)),
          pl.BlockSpec((None, None, 6, d), lambda bi, i: (layer, bi, 0, 0)),
          _const_spec((1, d)),
          _const_spec((groups, gdim, gdim)),
          _const_spec((1, d)),
      ],
      out_specs=pl.BlockSpec((None, tm, d), lambda bi, i: (bi, i, 0)),
      out_shape=jax.ShapeDtypeStruct(x.shape, F32),
      compiler_params=_params("arbitrary", "arbitrary"),
      name=f"pool_mixer_{layer}",
  )(x, x, mod, g.reshape(1, d), w.astype(BF16), ps.reshape(1, d))


def _sgu_kernel(x_ref, mod_ref, g_ref, win_ref, lng_ref, lnb_ref, ws_ref,
                bst_ref, wout_ref, o_ref, gated_scr, *, tm):
  g = g_ref[...]
  shift, scale, gate = mod_ref[0:1, :], mod_ref[1:2, :], mod_ref[2:3, :]
  x = x_ref[...]
  h = _norm_mod(x, g, shift, scale).astype(BF16)
  width = wout_ref.shape[0]
  u = _gelu(jnp.dot(h, win_ref[:, :width], preferred_element_type=F32))
  v = _gelu(jnp.dot(h, win_ref[:, width:], preferred_element_type=F32))
  mu = jnp.mean(v, axis=-1, keepdims=True)
  vc = v - mu
  var = jnp.mean(vc * vc, axis=-1, keepdims=True)
  v = (vc * lax.rsqrt(var + LN_EPS) * lng_ref[...] + lnb_ref[...]).astype(BF16)
  t = SGU_CHUNK
  causal = (lax.broadcasted_iota(jnp.int32, (t, t), 1)
            <= lax.broadcasted_iota(jnp.int32, (t, t), 0))
  heads = ws_ref.shape[0]
  for hd in range(heads):
    cols = slice(hd * SGU_HEAD_DIM, (hd + 1) * SGU_HEAD_DIM)
    ws = jnp.where(causal, ws_ref[hd], 0.0).astype(BF16)
    bias = bst_ref[:, hd:hd + 1]
    for c in range(tm // t):
      rows = slice(c * t, (c + 1) * t)
      mixed = jnp.dot(ws, v[rows, cols], preferred_element_type=F32) + bias
      gated_scr[rows, cols] = (u[rows, cols] * mixed).astype(BF16)
  y = jnp.dot(gated_scr[...], wout_ref[...], preferred_element_type=F32)
  o_ref[...] = x + gate * y


def _sgu_layer(x, mod, layer, g, w_in, ln_g, ln_b, w_s, b_s, w_out):
  b, s, d = x.shape
  tm = TOKEN_TILE
  width = w_out.shape[0]
  heads, t, _ = w_s.shape
  assert t == SGU_CHUNK and tm % t == 0 and heads * SGU_HEAD_DIM == width
  return pl.pallas_call(
      functools.partial(_sgu_kernel, tm=tm),
      grid=(b, s // tm),
      in_specs=[
          pl.BlockSpec((None, tm, d), lambda bi, i: (bi, i, 0)),
          pl.BlockSpec((None, None, 6, d), lambda bi, i: (layer, bi, 0, 0)),
          _const_spec((1, d)),
          _const_spec((d, 2 * width)),
          _const_spec((1, width)),
          _const_spec((1, width)),
          _const_spec((heads, t, t)),
          _const_spec((t, heads)),
          _const_spec((width, d)),
      ],
      out_specs=pl.BlockSpec((None, tm, d), lambda bi, i: (bi, i, 0)),
      out_shape=jax.ShapeDtypeStruct(x.shape, F32),
      scratch_shapes=[pltpu.VMEM((tm, width), BF16)],
      compiler_params=_params("arbitrary", "arbitrary"),
      name=f"sgu_mixer_{layer}",
  )(x, mod, g.reshape(1, d), w_in.astype(BF16), ln_g.reshape(1, width),
    ln_b.reshape(1, width), w_s, b_s.T, w_out.astype(BF16))


def _mla_fold_kernel(wqn_ref, wuk_ref, wuv_ref, wo_ref, qabs_ref, ovo_ref):
  qabs_ref[...] = lax.dot_general(
      wqn_ref[...], wuk_ref[...], (((1,), (1,)), ((), ())),
      precision=lax.Precision.HIGHEST, preferred_element_type=F32).astype(BF16)
  ovo_ref[...] = jnp.dot(wuv_ref[...], wo_ref[...],
                         precision=lax.Precision.HIGHEST,
                         preferred_element_type=F32).astype(BF16)


def _mla_fold(w_uq, w_ukv, w_o):
  h, ql, kvl = MLA_HEADS, MLA_Q_LORA, MLA_KV_LORA
  d = w_o.shape[1]
  w_uq3 = w_uq.reshape(ql, h, MLA_NOPE + MLA_ROPE)
  w_ukv3 = w_ukv.reshape(kvl, h, MLA_NOPE + MLA_V)
  wqn = w_uq3[:, :, :MLA_NOPE].transpose(1, 0, 2)
  wuk = w_ukv3[:, :, :MLA_NOPE].transpose(1, 0, 2)
  wuv = w_ukv3[:, :, MLA_NOPE:].transpose(1, 0, 2)
  wo = w_o.reshape(h, MLA_V, d)
  qabs, ovo = pl.pallas_call(
      _mla_fold_kernel,
      grid=(h,),
      in_specs=[
          pl.BlockSpec((None, ql, MLA_NOPE), lambda i: (i, 0, 0)),
          pl.BlockSpec((None, kvl, MLA_NOPE), lambda i: (i, 0, 0)),
          pl.BlockSpec((None, kvl, MLA_V), lambda i: (i, 0, 0)),
          pl.BlockSpec((None, MLA_V, d), lambda i: (i, 0, 0)),
      ],
      out_specs=[
          pl.BlockSpec((None, ql, kvl), lambda i: (i, 0, 0)),
          pl.BlockSpec((kvl, d), lambda i: (i, 0)),
      ],
      out_shape=[
          jax.ShapeDtypeStruct((h, ql, kvl), BF16),
          jax.ShapeDtypeStruct((h * kvl, d), BF16),
      ],
      compiler_params=_params("arbitrary"),
      name="mla_fold",
  )(wqn, wuk, wuv, wo)
  half = MLA_ROPE // 2
  wr = w_uq3[:, :, MLA_NOPE:]
  zeros = jnp.zeros((ql, h, LANES - MLA_ROPE), F32)
  wra = jnp.concatenate([wr, zeros], axis=-1)
  wrb = jnp.concatenate([wr[:, :, half:], wr[:, :, :half], zeros], axis=-1)
  w_q = jnp.concatenate([
      qabs.transpose(1, 0, 2).reshape(ql, h * kvl),
      wra.reshape(ql, h * LANES).astype(BF16),
      wrb.reshape(ql, h * LANES).astype(BF16),
  ], axis=1)
  return w_q, ovo


def _mla_proj_kernel(x_ref, mod_ref, g_ref, pos_ref, wlat_ref, qg_ref, kvg_ref,
                     wq_ref, freq_ref, sign_ref, q_ref, kv_ref, *, sm_scale):
  g = g_ref[...]
  shift, scale = mod_ref[0:1, :], mod_ref[1:2, :]
  h = _norm_mod(x_ref[...], g, shift, scale).astype(BF16)
  lat = jnp.dot(h, wlat_ref[...], preferred_element_type=F32)
  ql, kvl = MLA_Q_LORA, MLA_KV_LORA
  cq = (_rms(lat[:, :ql], RMS_EPS) * qg_ref[...]).astype(BF16)
  ckv = _rms(lat[:, ql:ql + kvl], RMS_EPS) * kvg_ref[...]
  ang = pos_ref[...].astype(F32) * freq_ref[...]
  cos = jnp.cos(ang)
  sin = jnp.sin(ang) * sign_ref[...]
  kra = lat[:, ql + kvl:ql + kvl + LANES]
  krb = lat[:, ql + kvl + LANES:]
  kv_ref[:, :LANES] = ckv.astype(BF16)
  kv_ref[:, LANES:] = (kra * cos + krb * sin).astype(BF16)
  heads = q_ref.shape[0]
  hw = heads * LANES
  cos2 = jnp.concatenate([cos, cos], axis=1)
  sin2 = jnp.concatenate([sin, sin], axis=1)
  for p in range(heads // 2):
    c0 = 2 * p * LANES
    qa = jnp.dot(cq, wq_ref[:, c0:c0 + 2 * LANES], preferred_element_type=F32)
    ra = jnp.dot(cq, wq_ref[:, hw + c0:hw + c0 + 2 * LANES],
                 preferred_element_type=F32)
    rb = jnp.dot(cq, wq_ref[:, 2 * hw + c0:2 * hw + c0 + 2 * LANES],
                 preferred_element_type=F32)
    qr = ra * cos2 + rb * sin2
    for k in range(2):
      lanes = slice(k * LANES, (k + 1) * LANES)
      q_ref[2 * p + k, :, :LANES] = (qa[:, lanes] * sm_scale).astype(BF16)
      q_ref[2 * p + k, :, LANES:] = (qr[:, lanes] * sm_scale).astype(BF16)


def _mla_proj(x, mod, layer, g, positions, w_dq_dkv, q_norm_g, kv_norm_g, w_q):
  b, s, d = x.shape
  tm = TOKEN_TILE
  ql, kvl, rope, h = MLA_Q_LORA, MLA_KV_LORA, MLA_ROPE, MLA_HEADS
  half = rope // 2
  wkr = w_dq_dkv[:, ql + kvl:]
  zeros = jnp.zeros((d, LANES - rope), F32)
  w_lat = jnp.concatenate(
      [w_dq_dkv[:, :ql + kvl], wkr, zeros, wkr[:, half:], wkr[:, :half], zeros],
      axis=1).astype(BF16)
  inv_freq = ROPE_THETA ** (-jnp.arange(0, rope, 2, dtype=F32) / rope)
  pad = jnp.zeros((LANES - rope,), F32)
  freq = jnp.concatenate([inv_freq, inv_freq, pad]).reshape(1, LANES)
  sign = jnp.concatenate([-jnp.ones((half,), F32),
                          jnp.ones((LANES - half,), F32)]).reshape(1, LANES)
  sm_scale = (MLA_NOPE + MLA_ROPE) ** -0.5
  return pl.pallas_call(
      functools.partial(_mla_proj_kernel, sm_scale=sm_scale),
      grid=(b, s // tm),
      in_specs=[
          pl.BlockSpec((None, tm, d), lambda bi, i: (bi, i, 0)),
          pl.BlockSpec((None, None, 6, d), lambda bi, i: (layer, bi, 0, 0)),
          _const_spec((1, d)),
          pl.BlockSpec((None, tm, 1), lambda bi, i: (bi, i, 0)),
          _const_spec(w_lat.shape),
          _const_spec((1, ql)),
          _const_spec((1, kvl)),
          _const_spec(w_q.shape),
          _const_spec((1, LANES)),
          _const_spec((1, LANES)),
      ],
      out_specs=[
          pl.BlockSpec((None, h, tm, QK_WIDTH), lambda bi, i: (bi, 0, i, 0)),
          pl.BlockSpec((None, tm, QK_WIDTH), lambda bi, i: (bi, i, 0)),
      ],
      out_shape=[
          jax.ShapeDtypeStruct((b, h, s, QK_WIDTH), BF16),
          jax.ShapeDtypeStruct((b, s, QK_WIDTH), BF16),
      ],
      compiler_params=_params("arbitrary", "arbitrary"),
      name=f"mla_proj_{layer}",
  )(x, mod, g.reshape(1, d), positions.reshape(b, s, 1), w_lat,
    q_norm_g.reshape(1, ql), kv_norm_g.reshape(1, kvl), w_q, freq, sign)


def _flash_kernel(q_ref, kv_ref, o_ref, m_scr, l_scr, acc_scr, *, tq, group):
  i = pl.program_id(1)
  heads = q_ref.shape[0]
  rows_per_group = group * tq
  m_scr[...] = jnp.full(m_scr.shape, -jnp.inf, F32)
  l_scr[...] = jnp.zeros(l_scr.shape, F32)
  acc_scr[...] = jnp.zeros(acc_scr.shape, F32)
  visible = (lax.broadcasted_iota(jnp.int32, (tq, tq), 1)
             <= lax.broadcasted_iota(jnp.int32, (tq, tq), 0))
  visible = jnp.concatenate([visible] * group, axis=0)

  def block(j, masked):
    kv = kv_ref[pl.ds(pl.multiple_of(j * tq, tq), tq), :]
    v = kv[:, :LANES]
    for gi in range(heads // group):
      rows = slice(gi * rows_per_group, (gi + 1) * rows_per_group)
      q = q_ref[gi * group:(gi + 1) * group].reshape(rows_per_group, QK_WIDTH)
      s = lax.dot_general(q, kv, (((1,), (1,)), ((), ())),
                          preferred_element_type=F32)
      if masked:
        s = jnp.where(visible, s, MASK_VALUE)
      m_prev = m_scr[rows]
      m_new = jnp.maximum(m_prev, jnp.max(s, axis=1, keepdims=True))
      alpha = jnp.exp(m_prev - m_new)
      p0 = jnp.exp(s[:, :LANES] - m_new)
      p1 = jnp.exp(s[:, LANES:] - m_new)
      l_scr[rows] = alpha * l_scr[rows] + (p0 + p1)
      p = jnp.concatenate([p0, p1], axis=1).astype(BF16)
      acc_scr[rows] = alpha * acc_scr[rows] + jnp.dot(
          p, v, preferred_element_type=F32)
      m_scr[rows] = m_new

  def full_block(j, carry):
    block(j, False)
    return carry

  lax.fori_loop(0, i, full_block, 0)
  block(i, True)
  for hd in range(heads):
    rows = slice(hd * tq, (hd + 1) * tq)
    l = jnp.sum(l_scr[rows], axis=1, keepdims=True)
    o_ref[:, hd * LANES:(hd + 1) * LANES] = (acc_scr[rows] / l).astype(BF16)


def _mla_attention(q, kv):
  b, h, s, w = q.shape
  tq = ATTN_TILE
  rows = h * tq
  return pl.pallas_call(
      functools.partial(_flash_kernel, tq=tq, group=ATTN_HEAD_GROUP),
      grid=(b, s // tq),
      in_specs=[
          pl.BlockSpec((None, h, tq, w), lambda bi, i: (bi, 0, i, 0)),
          pl.BlockSpec((None, s, w), lambda bi, i: (bi, 0, 0)),
      ],
      out_specs=pl.BlockSpec((None, tq, h * LANES), lambda bi, i: (bi, i, 0)),
      out_shape=jax.ShapeDtypeStruct((b, s, h * LANES), BF16),
      scratch_shapes=[pltpu.VMEM((rows, LANES), F32)] * 3,
      compiler_params=_params("arbitrary", "arbitrary"),
      name="mla_attention",
  )(q, kv)


def _mlp_kernel(*refs, tf, with_attn, final):
  refs = list(refs)
  x_ref, mod_ref, g_ref, w1_ref, w2_ref = refs[:5]
  rest = refs[5:]
  if with_attn:
    attn_ref, wo_ref = rest[:2]
    rest = rest[2:]
  if final:
    fg_ref = rest[0]
    rest = rest[1:]
  o_ref, h_scr, acc_scr = rest
  x = x_ref[...]
  if with_attn:
    y = jnp.dot(attn_ref[...], wo_ref[...], preferred_element_type=F32)
    x = x + mod_ref[2:3, :] * y
  h_scr[...] = _norm_mod(x, g_ref[...], mod_ref[3:4, :],
                         mod_ref[4:5, :]).astype(BF16)
  d_ff = w1_ref.shape[1]
  for j in range(d_ff // tf):
    cols = slice(j * tf, (j + 1) * tf)
    a = jnp.dot(h_scr[...], w1_ref[:, cols], preferred_element_type=F32)
    a = jnp.maximum(a, 0.0)
    part = jnp.dot((a * a).astype(BF16), w2_ref[cols, :],
                   preferred_element_type=F32)
    if j == 0:
      acc_scr[...] = part
    else:
      acc_scr[...] += part
  out = x + mod_ref[5:6, :] * acc_scr[...]
  if final:
    out = _rms(out, RMS_EPS) * fg_ref[...]
  o_ref[...] = out


def _mlp_layer(x, mod, layer, g, w1, w2, attn=None, w_ovo=None, final_g=None):
  b, s, d = x.shape
  tm = TOKEN_TILE
  d_ff = w1.shape[1]
  with_attn = attn is not None
  final = final_g is not None
  in_specs = [
      pl.BlockSpec((None, tm, d), lambda bi, i: (bi, i, 0)),
      pl.BlockSpec((None, None, 6, d), lambda bi, i: (layer, bi, 0, 0)),
      _const_spec((1, d)),
      _const_spec((d, d_ff)),
      _const_spec((d_ff, d)),
  ]
  args = [x, mod, g.reshape(1, d), w1.astype(BF16), w2.astype(BF16)]
  if with_attn:
    in_specs += [
        pl.BlockSpec((None, tm, attn.shape[-1]), lambda bi, i: (bi, i, 0)),
        _const_spec(w_ovo.shape),
    ]
    args += [attn, w_ovo]
  if final:
    in_specs.append(_const_spec((1, d)))
    args.append(final_g.reshape(1, d))
  return pl.pallas_call(
      functools.partial(_mlp_kernel, tf=MLP_FF_TILE, with_attn=with_attn,
                        final=final),
      grid=(b, s // tm),
      in_specs=in_specs,
      out_specs=pl.BlockSpec((None, tm, d), lambda bi, i: (bi, i, 0)),
      out_shape=jax.ShapeDtypeStruct(x.shape, F32),
      scratch_shapes=[pltpu.VMEM((tm, d), BF16), pltpu.VMEM((tm, d), F32)],
      compiler_params=_params("arbitrary", "arbitrary"),
      name=f"mlp_{layer}",
  )(*args)


def kernel(x, c, positions, ada_w, ada_b, norm_mix_g, norm_mlp_g, pool_w, pool_scale, sgu_w_in, sgu_ln_g, sgu_ln_b, sgu_w_s, sgu_b_s, sgu_w_out, mla_w_dq_dkv, mla_q_norm_g, mla_kv_norm_g, mla_w_uq, mla_w_ukv, mla_w_o, mlp_w1, mlp_w2, final_g):
  depth = ada_w.shape[0]
  mod = _ada_mod(c, ada_w, ada_b)
  for i in range(depth):
    kind, j = i % N_MIXERS, i // N_MIXERS
    attn = w_ovo = None
    if kind == 0:
      x = _pool_layer(x, mod, i, norm_mix_g[i], pool_w[j], pool_scale[j])
    elif kind == 1:
      x = _sgu_layer(x, mod, i, norm_mix_g[i], sgu_w_in[j], sgu_ln_g[j],
                     sgu_ln_b[j], sgu_w_s[j], sgu_b_s[j], sgu_w_out[j])
    else:
      w_q, w_ovo = _mla_fold(mla_w_uq[j], mla_w_ukv[j], mla_w_o[j])
      q, kv = _mla_proj(x, mod, i, norm_mix_g[i], positions, mla_w_dq_dkv[j],
                        mla_q_norm_g[j], mla_kv_norm_g[j], w_q)
      attn = _mla_attention(q, kv)
    x = _mlp_layer(x, mod, i, norm_mlp_g[i], mlp_w1[i], mlp_w2[i], attn=attn,
                   w_ovo=w_ovo,
                   final_g=final_g if i == depth - 1 else None)
  return x
```

```python
import functools

import jax
import jax.numpy as jnp
from jax import lax
from jax.experimental import pallas as pl
from jax.experimental.pallas import tpu as pltpu

F32 = jnp.float32
BF16 = jnp.bfloat16

POOL_WINDOWS = (2, 4, 8, 16)
SGU_CHUNK = 128
SGU_HEAD_DIM = 128
MLA_HEADS = 16
MLA_Q_LORA = 256
MLA_KV_LORA = 128
MLA_NOPE = 128
MLA_ROPE = 64
MLA_V = 128
ROPE_THETA = 10000.0
RMS_EPS = 1e-6
LN_EPS = 1e-5
N_MIXERS = 3
MASK_VALUE = -1e30
LOG2_E = 1.4426950408889634

LANES = 128
SUBLANES = 8
VMEM_LIMIT_BYTES = 56 * 1024 * 1024

TOKEN_TILE = 512
POOL_HALO = 16
ADA_COL_TILE = 2048
MLP_FF_TILE = 1024
ATTN_TILE = 512
ATTN_HEAD_GROUP = 4
QK_WIDTH = 2 * LANES


def _params(*semantics):
  return pltpu.CompilerParams(dimension_semantics=semantics,
                              vmem_limit_bytes=VMEM_LIMIT_BYTES)


def _const_spec(shape):
  zeros = (0,) * len(shape)
  return pl.BlockSpec(shape, lambda *_: zeros, pipeline_mode=pl.Buffered(1))


def _rms(x, eps):
  return x * lax.rsqrt(jnp.mean(x * x, axis=-1, keepdims=True) + eps)


def _norm_mod(x, g, shift, scale):
  return _rms(x, RMS_EPS) * g * (1.0 + scale) + shift


def _gelu(x):
  return 0.5 * x * (1.0 + lax.erf(x * (2.0 ** -0.5)))


def _ada_kernel(c_ref, w_ref, b_ref, o_ref):
  c = c_ref[...]
  ca = (c * jax.nn.sigmoid(c)).astype(BF16)
  o_ref[...] = jnp.dot(ca, w_ref[...].astype(BF16),
                       preferred_element_type=F32) + b_ref[...]


def _ada_mod(c, ada_w, ada_b):
  depth, d, d6 = ada_w.shape
  b = c.shape[0]
  tn = ADA_COL_TILE
  out = pl.pallas_call(
      _ada_kernel,
      grid=(depth, d6 // tn),
      in_specs=[
          pl.BlockSpec((b, d), lambda l, j: (0, 0)),
          pl.BlockSpec((None, d, tn), lambda l, j: (l, 0, j)),
          pl.BlockSpec((None, 1, tn), lambda l, j: (l, 0, j)),
      ],
      out_specs=pl.BlockSpec((None, b, tn), lambda l, j: (l, 0, j)),
      out_shape=jax.ShapeDtypeStruct((depth, b, d6), F32),
      compiler_params=_params("arbitrary", "arbitrary"),
      name="ada_mod",
  )(c, ada_w, ada_b.reshape(depth, 1, d6))
  return out.reshape(depth, b, 6, d)


def _pool_mix(x, xh, i, g, shift, scale, gate, w_ref, ps, tm):
  h = _norm_mod(x, g, shift, scale)
  hh = _norm_mod(xh, g, shift, scale)
  hh = jnp.where(i > 0, hh, 0.0)
  a = jnp.concatenate([hh, h], axis=0)
  t = (i * tm + lax.broadcasted_iota(jnp.int32, (tm, 1), 0)).astype(F32)
  gdim = w_ref.shape[1]
  outs = []
  for gi, win in enumerate(POOL_WINDOWS):
    cols = slice(gi * gdim, (gi + 1) * gdim)
    s = a[:, cols]
    k = 1
    while k < win:
      s = s + pltpu.roll(s, k, axis=0)
      k *= 2
    s = s[POOL_HALO:]
    inv_count = 1.0 / jnp.minimum(t + 1.0, float(win))
    pooled = s * inv_count - h[:, cols]
    y = jnp.dot(pooled.astype(BF16), w_ref[gi], preferred_element_type=F32)
    outs.append(x[:, cols] + gate[:, cols] * (y * ps[:, cols]))
  return jnp.concatenate(outs, axis=1)


def _sgu_kernel(x_ref, mod_ref, g_ref, win_ref, lng_ref, lnb_ref, ws_ref,
                bst_ref, wout_ref, o_ref, gated_scr, *, tm):
  g = g_ref[...]
  shift, scale, gate = mod_ref[0:1, :], mod_ref[1:2, :], mod_ref[2:3, :]
  x = x_ref[...]
  h = _norm_mod(x, g, shift, scale).astype(BF16)
  width = wout_ref.shape[0]
  u = _gelu(jnp.dot(h, win_ref[:, :width], preferred_element_type=F32))
  v = _gelu(jnp.dot(h, win_ref[:, width:], preferred_element_type=F32))
  mu = jnp.mean(v, axis=-1, keepdims=True)
  vc = v - mu
  var = jnp.mean(vc * vc, axis=-1, keepdims=True)
  v = (vc * lax.rsqrt(var + LN_EPS) * lng_ref[...] + lnb_ref[...]).astype(BF16)
  t = SGU_CHUNK
  causal = (lax.broadcasted_iota(jnp.int32, (t, t), 1)
            <= lax.broadcasted_iota(jnp.int32, (t, t), 0))
  heads = ws_ref.shape[0]
  for hd in range(heads):
    cols = slice(hd * SGU_HEAD_DIM, (hd + 1) * SGU_HEAD_DIM)
    ws = jnp.where(causal, ws_ref[hd], 0.0).astype(BF16)
    bias = bst_ref[:, hd:hd + 1]
    for c in range(tm // t):
      rows = slice(c * t, (c + 1) * t)
      mixed = jnp.dot(ws, v[rows, cols], preferred_element_type=F32) + bias
      gated_scr[rows, cols] = (u[rows, cols] * mixed).astype(BF16)
  y = jnp.dot(gated_scr[...], wout_ref[...], preferred_element_type=F32)
  o_ref[...] = x + gate * y


def _sgu_layer(x, mod, layer, g, w_in, ln_g, ln_b, w_s, b_s, w_out):
  b, s, d = x.shape
  tm = TOKEN_TILE
  width = w_out.shape[0]
  heads, t, _ = w_s.shape
  assert t == SGU_CHUNK and tm % t == 0 and heads * SGU_HEAD_DIM == width
  return pl.pallas_call(
      functools.partial(_sgu_kernel, tm=tm),
      grid=(b, s // tm),
      in_specs=[
          pl.BlockSpec((None, tm, d), lambda bi, i: (bi, i, 0)),
          pl.BlockSpec((None, None, 6, d), lambda bi, i: (layer, bi, 0, 0)),
          _const_spec((1, d)),
          _const_spec((d, 2 * width)),
          _const_spec((1, width)),
          _const_spec((1, width)),
          _const_spec((heads, t, t)),
          _const_spec((t, heads)),
          _const_spec((width, d)),
      ],
      out_specs=pl.BlockSpec((None, tm, d), lambda bi, i: (bi, i, 0)),
      out_shape=jax.ShapeDtypeStruct(x.shape, F32),
      scratch_shapes=[pltpu.VMEM((tm, width), BF16)],
      compiler_params=_params("arbitrary", "arbitrary"),
      name=f"sgu_mixer_{layer}",
  )(x, mod, g.reshape(1, d), w_in.astype(BF16), ln_g.reshape(1, width),
    ln_b.reshape(1, width), w_s, b_s.T, w_out.astype(BF16))


def _mla_fold_kernel(wqn_ref, wuk_ref, wuv_ref, wo_ref, qabs_ref, ovo_ref):
  qabs_ref[...] = lax.dot_general(
      wqn_ref[...], wuk_ref[...], (((1,), (1,)), ((), ())),
      precision=lax.Precision.HIGHEST, preferred_element_type=F32).astype(BF16)
  ovo_ref[...] = jnp.dot(wuv_ref[...], wo_ref[...],
                         precision=lax.Precision.HIGHEST,
                         preferred_element_type=F32).astype(BF16)


def _mla_fold(w_uq, w_ukv, w_o):
  h, ql, kvl = MLA_HEADS, MLA_Q_LORA, MLA_KV_LORA
  d = w_o.shape[1]
  w_uq3 = w_uq.reshape(ql, h, MLA_NOPE + MLA_ROPE)
  w_ukv3 = w_ukv.reshape(kvl, h, MLA_NOPE + MLA_V)
  wqn = w_uq3[:, :, :MLA_NOPE].transpose(1, 0, 2)
  wuk = w_ukv3[:, :, :MLA_NOPE].transpose(1, 0, 2)
  wuv = w_ukv3[:, :, MLA_NOPE:].transpose(1, 0, 2)
  wo = w_o.reshape(h, MLA_V, d)
  qabs, ovo = pl.pallas_call(
      _mla_fold_kernel,
      grid=(h,),
      in_specs=[
          pl.BlockSpec((None, ql, MLA_NOPE), lambda i: (i, 0, 0)),
          pl.BlockSpec((None, kvl, MLA_NOPE), lambda i: (i, 0, 0)),
          pl.BlockSpec((None, kvl, MLA_V), lambda i: (i, 0, 0)),
          pl.BlockSpec((None, MLA_V, d), lambda i: (i, 0, 0)),
      ],
      out_specs=[
          pl.BlockSpec((None, ql, kvl), lambda i: (i, 0, 0)),
          pl.BlockSpec((kvl, d), lambda i: (i, 0)),
      ],
      out_shape=[
          jax.ShapeDtypeStruct((h, ql, kvl), BF16),
          jax.ShapeDtypeStruct((h * kvl, d), BF16),
      ],
      compiler_params=_params("arbitrary"),
      name="mla_fold",
  )(wqn, wuk, wuv, wo)
  half = MLA_ROPE // 2
  wr = w_uq3[:, :, MLA_NOPE:]
  zeros = jnp.zeros((ql, h, LANES - MLA_ROPE), F32)
  wra = jnp.concatenate([wr, zeros], axis=-1)
  wrb = jnp.concatenate([wr[:, :, half:], wr[:, :, :half], zeros], axis=-1)
  w_q = jnp.concatenate([
      qabs.transpose(1, 0, 2).reshape(ql, h * kvl),
      wra.reshape(ql, h * LANES).astype(BF16),
      wrb.reshape(ql, h * LANES).astype(BF16),
  ], axis=1)
  return w_q, ovo


def _mla_proj_kernel(x_ref, mod_ref, g_ref, pos_ref, wlat_ref, qg_ref, kvg_ref,
                     wq_ref, freq_ref, sign_ref, q_ref, kv_ref, *, q_scale):
  g = g_ref[...]
  shift, scale = mod_ref[0:1, :], mod_ref[1:2, :]
  h = _norm_mod(x_ref[...], g, shift, scale).astype(BF16)
  lat = jnp.dot(h, wlat_ref[...], preferred_element_type=F32)
  ql, kvl = MLA_Q_LORA, MLA_KV_LORA
  cq = (_rms(lat[:, :ql], RMS_EPS) * qg_ref[...]).astype(BF16)
  ckv = _rms(lat[:, ql:ql + kvl], RMS_EPS) * kvg_ref[...]
  ang = pos_ref[...].astype(F32) * freq_ref[...]
  cos = jnp.cos(ang)
  sin = jnp.sin(ang) * sign_ref[...]
  kra = lat[:, ql + kvl:ql + kvl + LANES]
  krb = lat[:, ql + kvl + LANES:]
  kv_ref[:, :LANES] = ckv.astype(BF16)
  kv_ref[:, LANES:] = (kra * cos + krb * sin).astype(BF16)
  heads = q_ref.shape[0]
  hw = heads * LANES
  cos2 = jnp.concatenate([cos, cos], axis=1)
  sin2 = jnp.concatenate([sin, sin], axis=1)
  for p in range(heads // 2):
    c0 = 2 * p * LANES
    qa = jnp.dot(cq, wq_ref[:, c0:c0 + 2 * LANES], preferred_element_type=F32)
    ra = jnp.dot(cq, wq_ref[:, hw + c0:hw + c0 + 2 * LANES],
                 preferred_element_type=F32)
    rb = jnp.dot(cq, wq_ref[:, 2 * hw + c0:2 * hw + c0 + 2 * LANES],
                 preferred_element_type=F32)
    qr = ra * cos2 + rb * sin2
    for k in range(2):
      lanes = slice(k * LANES, (k + 1) * LANES)
      q_ref[2 * p + k, :, :LANES] = (qa[:, lanes] * q_scale).astype(BF16)
      q_ref[2 * p + k, :, LANES:] = (qr[:, lanes] * q_scale).astype(BF16)


def _mla_proj(x, mod, layer, g, positions, w_dq_dkv, q_norm_g, kv_norm_g, w_q):
  b, s, d = x.shape
  tm = TOKEN_TILE
  ql, kvl, rope, h = MLA_Q_LORA, MLA_KV_LORA, MLA_ROPE, MLA_HEADS
  half = rope // 2
  wkr = w_dq_dkv[:, ql + kvl:]
  zeros = jnp.zeros((d, LANES - rope), F32)
  w_lat = jnp.concatenate(
      [w_dq_dkv[:, :ql + kvl], wkr, zeros, wkr[:, half:], wkr[:, :half], zeros],
      axis=1).astype(BF16)
  inv_freq = ROPE_THETA ** (-jnp.arange(0, rope, 2, dtype=F32) / rope)
  pad = jnp.zeros((LANES - rope,), F32)
  freq = jnp.concatenate([inv_freq, inv_freq, pad]).reshape(1, LANES)
  sign = jnp.concatenate([-jnp.ones((half,), F32),
                          jnp.ones((LANES - half,), F32)]).reshape(1, LANES)
  q_scale = (MLA_NOPE + MLA_ROPE) ** -0.5 * LOG2_E
  return pl.pallas_call(
      functools.partial(_mla_proj_kernel, q_scale=q_scale),
      grid=(b, s // tm),
      in_specs=[
          pl.BlockSpec((None, tm, d), lambda bi, i: (bi, i, 0)),
          pl.BlockSpec((None, None, 6, d), lambda bi, i: (layer, bi, 0, 0)),
          _const_spec((1, d)),
          pl.BlockSpec((None, tm, 1), lambda bi, i: (bi, i, 0)),
          _const_spec(w_lat.shape),
          _const_spec((1, ql)),
          _const_spec((1, kvl)),
          _const_spec(w_q.shape),
          _const_spec((1, LANES)),
          _const_spec((1, LANES)),
      ],
      out_specs=[
          pl.BlockSpec((None, h, tm, QK_WIDTH), lambda bi, i: (bi, 0, i, 0)),
          pl.BlockSpec((None, tm, QK_WIDTH), lambda bi, i: (bi, i, 0)),
      ],
      out_shape=[
          jax.ShapeDtypeStruct((b, h, s, QK_WIDTH), BF16),
          jax.ShapeDtypeStruct((b, s, QK_WIDTH), BF16),
      ],
      compiler_params=_params("arbitrary", "arbitrary"),
      name=f"mla_proj_{layer}",
  )(x, mod, g.reshape(1, d), positions.reshape(b, s, 1), w_lat,
    q_norm_g.reshape(1, ql), kv_norm_g.reshape(1, kvl), w_q, freq, sign)


def _flash_kernel(q_ref, kv_ref, o_ref, m_scr, l_scr, acc_scr, *, tq, group):
  i = pl.program_id(1)
  heads = q_ref.shape[0]
  rows_per_group = group * tq
  m_scr[...] = jnp.full(m_scr.shape, -jnp.inf, F32)
  l_scr[...] = jnp.zeros(l_scr.shape, F32)
  acc_scr[...] = jnp.zeros(acc_scr.shape, F32)
  visible = (lax.broadcasted_iota(jnp.int32, (tq, tq), 1)
             <= lax.broadcasted_iota(jnp.int32, (tq, tq), 0))
  visible = jnp.concatenate([visible] * group, axis=0)

  def block(j, masked):
    kv = kv_ref[pl.ds(pl.multiple_of(j * tq, tq), tq), :]
    v = kv[:, :LANES]
    for gi in range(heads // group):
      rows = slice(gi * rows_per_group, (gi + 1) * rows_per_group)
      q = q_ref[gi * group:(gi + 1) * group].reshape(rows_per_group, QK_WIDTH)
      s = lax.dot_general(q, kv, (((1,), (1,)), ((), ())),
                          preferred_element_type=F32)
      if masked:
        s = jnp.where(visible, s, MASK_VALUE)
      m_prev = m_scr[rows]
      m_new = jnp.maximum(m_prev, jnp.max(s, axis=1, keepdims=True))
      alpha = jnp.exp2(m_prev - m_new)
      ps = [jnp.exp2(s[:, k * LANES:(k + 1) * LANES] - m_new)
            for k in range(tq // LANES)]
      l_scr[rows] = alpha * l_scr[rows] + functools.reduce(
          lambda a, b: a + b, ps)
      p = jnp.concatenate(ps, axis=1).astype(BF16)
      acc_scr[rows] = alpha * acc_scr[rows] + jnp.dot(
          p, v, preferred_element_type=F32)
      m_scr[rows] = m_new

  def full_block(j, carry):
    block(j, False)
    return carry

  lax.fori_loop(0, i, full_block, 0)
  block(i, True)
  for hd in range(heads):
    rows = slice(hd * tq, (hd + 1) * tq)
    l = jnp.sum(l_scr[rows], axis=1, keepdims=True)
    o_ref[:, hd * LANES:(hd + 1) * LANES] = (acc_scr[rows] / l).astype(BF16)


def _mla_attention(q, kv):
  b, h, s, w = q.shape
  tq = ATTN_TILE
  rows = h * tq
  return pl.pallas_call(
      functools.partial(_flash_kernel, tq=tq, group=ATTN_HEAD_GROUP),
      grid=(b, s // tq),
      in_specs=[
          pl.BlockSpec((None, h, tq, w), lambda bi, i: (bi, 0, i, 0)),
          pl.BlockSpec((None, s, w), lambda bi, i: (bi, 0, 0)),
      ],
      out_specs=pl.BlockSpec((None, tq, h * LANES), lambda bi, i: (bi, i, 0)),
      out_shape=jax.ShapeDtypeStruct((b, s, h * LANES), BF16),
      scratch_shapes=[pltpu.VMEM((rows, LANES), F32)] * 3,
      compiler_params=_params("arbitrary", "arbitrary"),
      name="mla_attention",
  )(q, kv)


def _mlp_kernel(*refs, tm, tf, pre, final):
  refs = list(refs)
  x_ref, mod_ref, g_ref, w1_ref, w2_ref = refs[:5]
  rest = refs[5:]
  if pre == "attn":
    attn_ref, wo_ref = rest[:2]
    rest = rest[2:]
  elif pre == "pool":
    xh_ref, gmix_ref, pw_ref, ps_ref = rest[:4]
    rest = rest[4:]
  if final:
    fg_ref = rest[0]
    rest = rest[1:]
  o_ref, h_scr, acc_scr = rest
  x = x_ref[...]
  if pre == "attn":
    y = jnp.dot(attn_ref[...], wo_ref[...], preferred_element_type=F32)
    x = x + mod_ref[2:3, :] * y
  elif pre == "pool":
    x = _pool_mix(x, xh_ref[...], pl.program_id(1), gmix_ref[...],
                  mod_ref[0:1, :], mod_ref[1:2, :], mod_ref[2:3, :], pw_ref,
                  ps_ref[...], tm)
  h_scr[...] = _norm_mod(x, g_ref[...], mod_ref[3:4, :],
                         mod_ref[4:5, :]).astype(BF16)
  d_ff = w1_ref.shape[1]
  for j in range(d_ff // tf):
    cols = slice(j * tf, (j + 1) * tf)
    a = jnp.dot(h_scr[...], w1_ref[:, cols], preferred_element_type=F32)
    a = jnp.maximum(a, 0.0)
    part = jnp.dot((a * a).astype(BF16), w2_ref[cols, :],
                   preferred_element_type=F32)
    if j == 0:
      acc_scr[...] = part
    else:
      acc_scr[...] += part
  out = x + mod_ref[5:6, :] * acc_scr[...]
  if final:
    out = _rms(out, RMS_EPS) * fg_ref[...]
  o_ref[...] = out


def _mlp_layer(x, mod, layer, g, w1, w2, attn=None, pool=None, final_g=None):
  b, s, d = x.shape
  tm = TOKEN_TILE
  d_ff = w1.shape[1]
  pre = "attn" if attn is not None else "pool" if pool is not None else None
  final = final_g is not None
  in_specs = [
      pl.BlockSpec((None, tm, d), lambda bi, i: (bi, i, 0)),
      pl.BlockSpec((None, None, 6, d), lambda bi, i: (layer, bi, 0, 0)),
      _const_spec((1, d)),
      _const_spec((d, d_ff)),
      _const_spec((d_ff, d)),
  ]
  args = [x, mod, g.reshape(1, d), w1.astype(BF16), w2.astype(BF16)]
  if pre == "attn":
    attn_out, w_ovo = attn
    in_specs += [
        pl.BlockSpec((None, tm, attn_out.shape[-1]), lambda bi, i: (bi, i, 0)),
        _const_spec(w_ovo.shape),
    ]
    args += [attn_out, w_ovo]
  elif pre == "pool":
    g_mix, pool_w, pool_scale = pool
    assert POOL_HALO >= max(POOL_WINDOWS) and tm % POOL_HALO == 0
    halo_blocks_per_tile = tm // POOL_HALO
    in_specs += [
        pl.BlockSpec(
            (None, POOL_HALO, d),
            lambda bi, i: (bi, jnp.maximum(i * halo_blocks_per_tile - 1, 0), 0)),
        _const_spec((1, d)),
        _const_spec(pool_w.shape),
        _const_spec((1, d)),
    ]
    args += [x, g_mix.reshape(1, d), pool_w.astype(BF16),
             pool_scale.reshape(1, d)]
  if final:
    in_specs.append(_const_spec((1, d)))
    args.append(final_g.reshape(1, d))
  return pl.pallas_call(
      functools.partial(_mlp_kernel, tm=tm, tf=MLP_FF_TILE, pre=pre,
                        final=final),
      grid=(b, s // tm),
      in_specs=in_specs,
      out_specs=pl.BlockSpec((None, tm, d), lambda bi, i: (bi, i, 0)),
      out_shape=jax.ShapeDtypeStruct(x.shape, F32),
      scratch_shapes=[pltpu.VMEM((tm, d), BF16), pltpu.VMEM((tm, d), F32)],
      compiler_params=_params("arbitrary", "arbitrary"),
      name=f"mlp_{layer}",
  )(*args)


def kernel(x, c, positions, ada_w, ada_b, norm_mix_g, norm_mlp_g, pool_w, pool_scale, sgu_w_in, sgu_ln_g, sgu_ln_b, sgu_w_s, sgu_b_s, sgu_w_out, mla_w_dq_dkv, mla_q_norm_g, mla_kv_norm_g, mla_w_uq, mla_w_ukv, mla_w_o, mlp_w1, mlp_w2, final_g):
  depth = ada_w.shape[0]
  mod = _ada_mod(c, ada_w, ada_b)
  for i in range(depth):
    kind, j = i % N_MIXERS, i // N_MIXERS
    attn = pool = None
    if kind == 0:
      pool = (norm_mix_g[i], pool_w[j], pool_scale[j])
    elif kind == 1:
      x = _sgu_layer(x, mod, i, norm_mix_g[i], sgu_w_in[j], sgu_ln_g[j],
                     sgu_ln_b[j], sgu_w_s[j], sgu_b_s[j], sgu_w_out[j])
    else:
      w_q, w_ovo = _mla_fold(mla_w_uq[j], mla_w_ukv[j], mla_w_o[j])
      q, kv = _mla_proj(x, mod, i, norm_mix_g[i], positions, mla_w_dq_dkv[j],
                        mla_q_norm_g[j], mla_kv_norm_g[j], w_q)
      attn = (_mla_attention(q, kv), w_ovo)
    x = _mlp_layer(x, mod, i, norm_mlp_g[i], mlp_w1[i], mlp_w2[i], attn=attn,
                   pool=pool, final_g=final_g if i == depth - 1 else None)
  return x
```

```python
import functools

import jax
import jax.numpy as jnp
from jax import lax
from jax.experimental import pallas as pl
from jax.experimental.pallas import tpu as pltpu

F32 = jnp.float32
BF16 = jnp.bfloat16

POOL_WINDOWS = (2, 4, 8, 16)
SGU_CHUNK = 128
SGU_HEAD_DIM = 128
MLA_HEADS = 16
MLA_Q_LORA = 256
MLA_KV_LORA = 128
MLA_NOPE = 128
MLA_ROPE = 64
MLA_V = 128
ROPE_THETA = 10000.0
RMS_EPS = 1e-6
LN_EPS = 1e-5
N_MIXERS = 3
MASK_VALUE = -1e30
LOG2_E = 1.4426950408889634
Q_SCALE = (MLA_NOPE + MLA_ROPE) ** -0.5 * LOG2_E

LANES = 128
SUBLANES = 8
VMEM_LIMIT_BYTES = 56 * 1024 * 1024

TOKEN_TILE = 512
SGU_TOKEN_TILE = 1024
PROJ_TOKEN_TILE = 1024
POOL_HALO = 16
ADA_COL_TILE = 2048
MLP_FF_TILE = 1024
ATTN_TILE = 512
ATTN_HEAD_GROUP = 4
QK_WIDTH = 2 * LANES


def _params(*semantics):
  return pltpu.CompilerParams(dimension_semantics=semantics,
                              vmem_limit_bytes=VMEM_LIMIT_BYTES)


def _const_spec(shape):
  zeros = (0,) * len(shape)
  return pl.BlockSpec(shape, lambda *_: zeros, pipeline_mode=pl.Buffered(1))


def _rms(x, eps):
  return x * lax.rsqrt(jnp.mean(x * x, axis=-1, keepdims=True) + eps)


def _norm_mod(x, g, shift, scale):
  return _rms(x, RMS_EPS) * g * (1.0 + scale) + shift


def _gelu(x):
  return 0.5 * x * (1.0 + lax.erf(x * (2.0 ** -0.5)))


def _ada_kernel(c_ref, w_ref, b_ref, o_ref):
  c = c_ref[...]
  ca = (c * jax.nn.sigmoid(c)).astype(BF16)
  o_ref[...] = jnp.dot(ca, w_ref[...].astype(BF16),
                       preferred_element_type=F32) + b_ref[...]


def _ada_mod(c, ada_w, ada_b):
  depth, d, d6 = ada_w.shape
  b = c.shape[0]
  tn = ADA_COL_TILE
  out = pl.pallas_call(
      _ada_kernel,
      grid=(depth, d6 // tn),
      in_specs=[
          pl.BlockSpec((b, d), lambda l, j: (0, 0)),
          pl.BlockSpec((None, d, tn), lambda l, j: (l, 0, j)),
          pl.BlockSpec((None, 1, tn), lambda l, j: (l, 0, j)),
      ],
      out_specs=pl.BlockSpec((None, b, tn), lambda l, j: (l, 0, j)),
      out_shape=jax.ShapeDtypeStruct((depth, b, d6), F32),
      compiler_params=_params("arbitrary", "arbitrary"),
      name="ada_mod",
  )(c, ada_w, ada_b.reshape(depth, 1, d6))
  return out.reshape(depth, b, 6, d)


def _pool_stages(x_ref, xh, i, g_ref, mod_ref, w_ref, ps_ref, a_scr, out_ref,
                 tm):
  def prep():
    g, shift, scale = g_ref[...], mod_ref[0:1, :], mod_ref[1:2, :]
    hh = _norm_mod(xh(), g, shift, scale)
    a_scr[0:POOL_HALO, :] = jnp.where(i > 0, hh, 0.0)
    a_scr[POOL_HALO:, :] = _norm_mod(x_ref[...], g, shift, scale)

  def group(gi, win):
    gdim = w_ref.shape[1]
    cols = slice(gi * gdim, (gi + 1) * gdim)
    s = a_scr[:, cols]
    k = 1
    while k < win:
      s = s + pltpu.roll(s, k, axis=0)
      k *= 2
    t = (i * tm + lax.broadcasted_iota(jnp.int32, (tm, 1), 0)).astype(F32)
    inv_count = 1.0 / jnp.minimum(t + 1.0, float(win))
    pooled = s[POOL_HALO:] * inv_count - a_scr[POOL_HALO:, cols]
    y = jnp.dot(pooled.astype(BF16), w_ref[gi], preferred_element_type=F32)
    out_ref[:, cols] = x_ref[:, cols] + mod_ref[2:3, cols] * (
        y * ps_ref[:, cols])

  return [prep] + [functools.partial(group, gi, win)
                   for gi, win in enumerate(POOL_WINDOWS)]


def _staggered(stages_a, stages_b):
  n = len(stages_a)
  for k in range(n + 1):
    if k < n:
      stages_a[k]()
    if k >= 1:
      stages_b[k - 1]()


def _sgu_stages(rows, n_rows, x_ref, mod_ref, g_ref, win_ref, lng_ref, lnb_ref,
                ws_ref, bst_ref, wout_ref, o_ref, gated_scr):
  st = {}
  width = wout_ref.shape[0]
  t = SGU_CHUNK

  def project_v():
    st["h"] = _norm_mod(x_ref[rows, :], g_ref[...], mod_ref[0:1, :],
                        mod_ref[1:2, :]).astype(BF16)
    st["v"] = jnp.dot(st["h"], win_ref[:, width:], preferred_element_type=F32)

  def project_u():
    st["u"] = _gelu(jnp.dot(st["h"], win_ref[:, :width],
                            preferred_element_type=F32))

  def normalise():
    v = _gelu(st["v"])
    mu = jnp.mean(v, axis=-1, keepdims=True)
    vc = v - mu
    var = jnp.mean(vc * vc, axis=-1, keepdims=True)
    st["v"] = (vc * lax.rsqrt(var + LN_EPS) * lng_ref[...]
               + lnb_ref[...]).astype(BF16)

  def gate_spatial():
    u, v = st["u"], st["v"]
    causal = (lax.broadcasted_iota(jnp.int32, (t, t), 1)
              <= lax.broadcasted_iota(jnp.int32, (t, t), 0))
    for hd in range(ws_ref.shape[0]):
      cols = slice(hd * SGU_HEAD_DIM, (hd + 1) * SGU_HEAD_DIM)
      ws = jnp.where(causal, ws_ref[hd], 0.0).astype(BF16)
      bias = bst_ref[:, hd:hd + 1]
      for c in range(n_rows // t):
        blk = slice(c * t, (c + 1) * t)
        mixed = jnp.dot(ws, v[blk, cols], preferred_element_type=F32) + bias
        gated_scr[rows.start + c * t:rows.start + (c + 1) * t, cols] = (
            u[blk, cols] * mixed).astype(BF16)

  def project_out():
    y = jnp.dot(gated_scr[rows, :], wout_ref[...], preferred_element_type=F32)
    o_ref[rows, :] = x_ref[rows, :] + mod_ref[2:3, :] * y

  return [project_v, project_u, normalise, gate_spatial, project_out]


def _sgu_kernel(*refs, tm):
  half = tm // 2
  _staggered(_sgu_stages(slice(0, half), half, *refs),
             _sgu_stages(slice(half, tm), half, *refs))


def _sgu_layer(x, mod, layer, g, w_in, ln_g, ln_b, w_s, b_s, w_out):
  b, s, d = x.shape
  tm = SGU_TOKEN_TILE
  width = w_out.shape[0]
  heads, t, _ = w_s.shape
  assert t == SGU_CHUNK and tm % t == 0 and heads * SGU_HEAD_DIM == width
  return pl.pallas_call(
      functools.partial(_sgu_kernel, tm=tm),
      grid=(b, s // tm),
      in_specs=[
          pl.BlockSpec((None, tm, d), lambda bi, i: (bi, i, 0)),
          pl.BlockSpec((None, None, 6, d), lambda bi, i: (layer, bi, 0, 0)),
          _const_spec((1, d)),
          _const_spec((d, 2 * width)),
          _const_spec((1, width)),
          _const_spec((1, width)),
          _const_spec((heads, t, t)),
          _const_spec((t, heads)),
          _const_spec((width, d)),
      ],
      out_specs=pl.BlockSpec((None, tm, d), lambda bi, i: (bi, i, 0)),
      out_shape=jax.ShapeDtypeStruct(x.shape, F32),
      scratch_shapes=[pltpu.VMEM((tm, width), BF16)],
      compiler_params=_params("arbitrary", "arbitrary"),
      name=f"sgu_mixer_{layer}",
  )(x, mod, g.reshape(1, d), w_in.astype(BF16), ln_g.reshape(1, width),
    ln_b.reshape(1, width), w_s, b_s.T, w_out.astype(BF16))


def _mla_fold_kernel(wqn_ref, wuk_ref, wuv_ref, wo_ref, qabs_ref, ovo_ref):
  qabs = lax.dot_general(
      wqn_ref[...], wuk_ref[...], (((1,), (1,)), ((), ())),
      precision=lax.Precision.HIGHEST, preferred_element_type=F32)
  qabs_ref[...] = (qabs * Q_SCALE).astype(BF16)
  ovo_ref[...] = jnp.dot(wuv_ref[...], wo_ref[...],
                         precision=lax.Precision.HIGHEST,
                         preferred_element_type=F32).astype(BF16)


def _mla_fold(w_uq, w_ukv, w_o):
  h, ql, kvl = MLA_HEADS, MLA_Q_LORA, MLA_KV_LORA
  d = w_o.shape[1]
  w_uq3 = w_uq.reshape(ql, h, MLA_NOPE + MLA_ROPE)
  w_ukv3 = w_ukv.reshape(kvl, h, MLA_NOPE + MLA_V)
  wqn = w_uq3[:, :, :MLA_NOPE].transpose(1, 0, 2)
  wuk = w_ukv3[:, :, :MLA_NOPE].transpose(1, 0, 2)
  wuv = w_ukv3[:, :, MLA_NOPE:].transpose(1, 0, 2)
  wo = w_o.reshape(h, MLA_V, d)
  qabs, ovo = pl.pallas_call(
      _mla_fold_kernel,
      grid=(h,),
      in_specs=[
          pl.BlockSpec((None, ql, MLA_NOPE), lambda i: (i, 0, 0)),
          pl.BlockSpec((None, kvl, MLA_NOPE), lambda i: (i, 0, 0)),
          pl.BlockSpec((None, kvl, MLA_V), lambda i: (i, 0, 0)),
          pl.BlockSpec((None, MLA_V, d), lambda i: (i, 0, 0)),
      ],
      out_specs=[
          pl.BlockSpec((None, ql, kvl), lambda i: (i, 0, 0)),
          pl.BlockSpec((kvl, d), lambda i: (i, 0)),
      ],
      out_shape=[
          jax.ShapeDtypeStruct((h, ql, kvl), BF16),
          jax.ShapeDtypeStruct((h * kvl, d), BF16),
      ],
      compiler_params=_params("arbitrary"),
      name="mla_fold",
  )(wqn, wuk, wuv, wo)
  half = MLA_ROPE // 2
  wr = w_uq3[:, :, MLA_NOPE:] * Q_SCALE
  zeros = jnp.zeros((ql, h, LANES - MLA_ROPE), F32)
  wra = jnp.concatenate([wr, zeros], axis=-1)
  wrb = jnp.concatenate([wr[:, :, half:], wr[:, :, :half], zeros], axis=-1)
  w_q = jnp.concatenate([
      qabs.transpose(1, 0, 2).reshape(ql, h * kvl),
      wra.reshape(ql, h * LANES).astype(BF16),
      wrb.reshape(ql, h * LANES).astype(BF16),
  ], axis=1)
  return w_q, ovo


def _mla_proj_stages(rows, x_ref, mod_ref, g_ref, pos_ref, wlat_ref, qg_ref,
                     kvg_ref, wq_ref, freq_ref, sign_ref, q_ref, kv_ref):
  st = {}
  ql, kvl = MLA_Q_LORA, MLA_KV_LORA
  n = rows.stop - rows.start
  mid = rows.start + n // 2
  heads = q_ref.shape[0]
  hw = heads * LANES

  def project_latent():
    h = _norm_mod(x_ref[rows, :], g_ref[...], mod_ref[0:1, :],
                  mod_ref[1:2, :]).astype(BF16)
    st["lat"] = jnp.dot(h, wlat_ref[...], preferred_element_type=F32)

  def rotary_and_keys():
    lat = st["lat"]
    st["cq"] = (_rms(lat[:, :ql], RMS_EPS) * qg_ref[...]).astype(BF16)
    ckv = _rms(lat[:, ql:ql + kvl], RMS_EPS) * kvg_ref[...]
    lane = lax.broadcasted_iota(jnp.int32, (1, LANES), 1)
    pos = jnp.where(lane < LANES // 2, pos_ref[rows.start:mid, :],
                    pos_ref[mid:rows.stop, :])
    ang = pos.astype(F32) * freq_ref[...]
    cos_p, sin_p = jnp.cos(ang), jnp.sin(ang)
    cos = jnp.concatenate([cos_p, pltpu.roll(cos_p, LANES // 2, axis=1)], axis=0)
    sin = jnp.concatenate([sin_p, pltpu.roll(sin_p, LANES // 2, axis=1)], axis=0)
    sin = sin * sign_ref[...]
    kra = lat[:, ql + kvl:ql + kvl + LANES]
    krb = lat[:, ql + kvl + LANES:]
    kv_ref[rows, :LANES] = ckv.astype(BF16)
    kv_ref[rows, LANES:] = (kra * cos + krb * sin).astype(BF16)
    st["cos2"] = jnp.concatenate([cos, cos], axis=1)
    st["sin2"] = jnp.concatenate([sin, sin], axis=1)

  def queries(pairs):
    cq = st["cq"]
    for p in pairs:
      c0 = 2 * p * LANES
      qa = jnp.dot(cq, wq_ref[:, c0:c0 + 2 * LANES],
                   preferred_element_type=F32)
      ra = jnp.dot(cq, wq_ref[:, hw + c0:hw + c0 + 2 * LANES],
                   preferred_element_type=F32)
      rb = jnp.dot(cq, wq_ref[:, 2 * hw + c0:2 * hw + c0 + 2 * LANES],
                   preferred_element_type=F32)
      qr = ra * st["cos2"] + rb * st["sin2"]
      for k in range(2):
        lanes = slice(k * LANES, (k + 1) * LANES)
        q_ref[2 * p + k, rows, :LANES] = qa[:, lanes].astype(BF16)
        q_ref[2 * p + k, rows, LANES:] = qr[:, lanes].astype(BF16)

  n_pairs = heads // 2
  return [project_latent, rotary_and_keys,
          functools.partial(queries, range(0, n_pairs // 2)),
          functools.partial(queries, range(n_pairs // 2, n_pairs))]


def _mla_proj_kernel(*refs, tm):
  half = tm // 2
  _staggered(_mla_proj_stages(slice(0, half), *refs),
             _mla_proj_stages(slice(half, tm), *refs))


def _mla_proj(x, mod, layer, g, positions, w_dq_dkv, q_norm_g, kv_norm_g, w_q):
  b, s, d = x.shape
  tm = PROJ_TOKEN_TILE
  ql, kvl, rope, h = MLA_Q_LORA, MLA_KV_LORA, MLA_ROPE, MLA_HEADS
  half = rope // 2
  assert 2 * rope == LANES
  wkr = w_dq_dkv[:, ql + kvl:]
  zeros = jnp.zeros((d, LANES - rope), F32)
  w_lat = jnp.concatenate(
      [w_dq_dkv[:, :ql + kvl], wkr, zeros, wkr[:, half:], wkr[:, :half], zeros],
      axis=1).astype(BF16)
  inv_freq = ROPE_THETA ** (-jnp.arange(0, rope, 2, dtype=F32) / rope)
  freq = jnp.tile(inv_freq, LANES // half).reshape(1, LANES)
  sign = jnp.concatenate([-jnp.ones((half,), F32),
                          jnp.ones((LANES - half,), F32)]).reshape(1, LANES)
  return pl.pallas_call(
      functools.partial(_mla_proj_kernel, tm=tm),
      grid=(b, s // tm),
      in_specs=[
          pl.BlockSpec((None, tm, d), lambda bi, i: (bi, i, 0)),
          pl.BlockSpec((None, None, 6, d), lambda bi, i: (layer, bi, 0, 0)),
          _const_spec((1, d)),
          pl.BlockSpec((None, tm, 1), lambda bi, i: (bi, i, 0)),
          _const_spec(w_lat.shape),
          _const_spec((1, ql)),
          _const_spec((1, kvl)),
          _const_spec(w_q.shape),
          _const_spec((1, LANES)),
          _const_spec((1, LANES)),
      ],
      out_specs=[
          pl.BlockSpec((None, h, tm, QK_WIDTH), lambda bi, i: (bi, 0, i, 0)),
          pl.BlockSpec((None, tm, QK_WIDTH), lambda bi, i: (bi, i, 0)),
      ],
      out_shape=[
          jax.ShapeDtypeStruct((b, h, s, QK_WIDTH), BF16),
          jax.ShapeDtypeStruct((b, s, QK_WIDTH), BF16),
      ],
      compiler_params=_params("arbitrary", "arbitrary"),
      name=f"mla_proj_{layer}",
  )(x, mod, g.reshape(1, d), positions.reshape(b, s, 1), w_lat,
    q_norm_g.reshape(1, ql), kv_norm_g.reshape(1, kvl), w_q, freq, sign)


def _flash_kernel(q_ref, kv_ref, o_ref, m_scr, l_scr, acc_scr, *, tq, group):
  i = pl.program_id(1)
  heads = q_ref.shape[0]
  rows_per_group = group * tq
  m_scr[...] = jnp.full(m_scr.shape, -jnp.inf, F32)
  l_scr[...] = jnp.zeros(l_scr.shape, F32)
  acc_scr[...] = jnp.zeros(acc_scr.shape, F32)
  visible = (lax.broadcasted_iota(jnp.int32, (tq, tq), 1)
             <= lax.broadcasted_iota(jnp.int32, (tq, tq), 0))
  visible = jnp.concatenate([visible] * group, axis=0)

  def block(j, masked):
    kv = kv_ref[pl.ds(pl.multiple_of(j * tq, tq), tq), :]
    v = kv[:, :LANES]
    for gi in range(heads // group):
      rows = slice(gi * rows_per_group, (gi + 1) * rows_per_group)
      q = q_ref[gi * group:(gi + 1) * group].reshape(rows_per_group, QK_WIDTH)
      s = lax.dot_general(q, kv, (((1,), (1,)), ((), ())),
                          preferred_element_type=F32)
      if masked:
        s = jnp.where(visible, s, MASK_VALUE)
      m_prev = m_scr[rows]
      m_new = jnp.maximum(m_prev, jnp.max(s, axis=1, keepdims=True))
      alpha = jnp.exp2(m_prev - m_new)
      ps = [jnp.exp2(s[:, k * LANES:(k + 1) * LANES] - m_new)
            for k in range(tq // LANES)]
      l_scr[rows] = alpha * l_scr[rows] + functools.reduce(
          lambda a, b: a + b, ps)
      p = jnp.concatenate(ps, axis=1).astype(BF16)
      acc_scr[rows] = alpha * acc_scr[rows] + jnp.dot(
          p, v, preferred_element_type=F32)
      m_scr[rows] = m_new

  def full_block(j, carry):
    block(j, False)
    return carry

  lax.fori_loop(0, i, full_block, 0)
  block(i, True)
  for hd in range(heads):
    rows = slice(hd * tq, (hd + 1) * tq)
    l = jnp.sum(l_scr[rows], axis=1, keepdims=True)
    o_ref[:, hd * LANES:(hd + 1) * LANES] = (acc_scr[rows] / l).astype(BF16)


def _mla_attention(q, kv):
  b, h, s, w = q.shape
  tq = ATTN_TILE
  rows = h * tq
  return pl.pallas_call(
      functools.partial(_flash_kernel, tq=tq, group=ATTN_HEAD_GROUP),
      grid=(b, s // tq),
      in_specs=[
          pl.BlockSpec((None, h, tq, w), lambda bi, i: (bi, 0, i, 0)),
          pl.BlockSpec((None, s, w), lambda bi, i: (bi, 0, 0)),
      ],
      out_specs=pl.BlockSpec((None, tq, h * LANES), lambda bi, i: (bi, i, 0)),
      out_shape=jax.ShapeDtypeStruct((b, s, h * LANES), BF16),
      scratch_shapes=[pltpu.VMEM((rows, LANES), F32)] * 3,
      compiler_params=_params("arbitrary", "arbitrary"),
      name="mla_attention",
  )(q, kv)


def _mlp_core(x, mod_ref, g_ref, w1_ref, w2_ref, fg_ref, h_scr, acc_scr, tf,
              side_work=()):
  h_scr[...] = _norm_mod(x, g_ref[...], mod_ref[3:4, :],
                         mod_ref[4:5, :]).astype(BF16)
  d_ff = w1_ref.shape[1]
  n_chunks = d_ff // tf
  side_work = list(side_work)
  for j in range(n_chunks):
    cols = slice(j * tf, (j + 1) * tf)
    a = jnp.dot(h_scr[...], w1_ref[:, cols], preferred_element_type=F32)
    a = jnp.maximum(a, 0.0)
    part = jnp.dot((a * a).astype(BF16), w2_ref[cols, :],
                   preferred_element_type=F32)
    if j == 0:
      acc_scr[...] = part
    else:
      acc_scr[...] += part
    slots = max(n_chunks - 1, 1)
    lo = min(-(-j * len(side_work) // slots), len(side_work))
    hi = min(-(-(j + 1) * len(side_work) // slots), len(side_work))
    for work in side_work[lo:hi]:
      work()
  out = x + mod_ref[5:6, :] * acc_scr[...]
  if fg_ref is not None:
    out = _rms(out, RMS_EPS) * fg_ref[...]
  return out


def _mlp_kernel(*refs, tf, with_attn, final):
  refs = list(refs)
  x_ref, mod_ref, g_ref, w1_ref, w2_ref = refs[:5]
  rest = refs[5:]
  if with_attn:
    attn_ref, wo_ref = rest[:2]
    rest = rest[2:]
  fg_ref = None
  if final:
    fg_ref = rest[0]
    rest = rest[1:]
  o_ref, h_scr, acc_scr = rest
  x = x_ref[...]
  if with_attn:
    y = jnp.dot(attn_ref[...], wo_ref[...], preferred_element_type=F32)
    x = x + mod_ref[2:3, :] * y
  o_ref[...] = _mlp_core(x, mod_ref, g_ref, w1_ref, w2_ref, fg_ref, h_scr,
                         acc_scr, tf)


def _layer_spec(shape, layer):
  zeros = (0,) * len(shape)
  return pl.BlockSpec((None,) + tuple(shape), lambda *_: (layer,) + zeros,
                      pipeline_mode=pl.Buffered(1))


def _mlp_layer(x, mod, layer, g, w1, w2, attn=None, final_g=None):
  b, s, d = x.shape
  tm = TOKEN_TILE
  d_ff = w1.shape[-1]
  with_attn = attn is not None
  final = final_g is not None
  in_specs = [
      pl.BlockSpec((None, tm, d), lambda bi, i: (bi, i, 0)),
      pl.BlockSpec((None, None, 6, d), lambda bi, i: (layer, bi, 0, 0)),
      _const_spec((1, d)),
      _layer_spec((d, d_ff), layer),
      _layer_spec((d_ff, d), layer),
  ]
  args = [x, mod, g.reshape(1, d), w1, w2]
  if with_attn:
    attn_out, w_ovo = attn
    in_specs += [
        pl.BlockSpec((None, tm, attn_out.shape[-1]), lambda bi, i: (bi, i, 0)),
        _const_spec(w_ovo.shape),
    ]
    args += [attn_out, w_ovo]
  if final:
    in_specs.append(_const_spec((1, d)))
    args.append(final_g.reshape(1, d))
  return pl.pallas_call(
      functools.partial(_mlp_kernel, tf=MLP_FF_TILE, with_attn=with_attn,
                        final=final),
      grid=(b, s // tm),
      in_specs=in_specs,
      out_specs=pl.BlockSpec((None, tm, d), lambda bi, i: (bi, i, 0)),
      out_shape=jax.ShapeDtypeStruct(x.shape, F32),
      scratch_shapes=[pltpu.VMEM((tm, d), BF16), pltpu.VMEM((tm, d), F32)],
      compiler_params=_params("arbitrary", "arbitrary"),
      name=f"mlp_{layer}",
  )(*args)


def _pool_mlp_kernel(*refs, tm, tf, tiles_per_seq, final):
  refs = list(refs)
  (x0_ref, xn_ref, xhn_ref, mod_ref, modn_ref, gmix_ref, pw_ref, ps_ref,
   g_ref, w1_ref, w2_ref) = refs[:11]
  rest = refs[11:]
  fg_ref = None
  if final:
    fg_ref = rest[0]
    rest = rest[1:]
  o_ref, x1_scr, a_scr, h_scr, acc_scr = rest
  t = pl.program_id(0)
  slot = t % 2

  @pl.when(t == 0)
  def _():
    for stage in _pool_stages(x0_ref, lambda: x0_ref[0:POOL_HALO, :], 0,
                              gmix_ref, mod_ref, pw_ref, ps_ref, a_scr,
                              x1_scr.at[0], tm):
      stage()

  next_stages = _pool_stages(xn_ref, lambda: xhn_ref[...],
                             (t + 1) % tiles_per_seq, gmix_ref, modn_ref,
                             pw_ref, ps_ref, a_scr, x1_scr.at[1 - slot], tm)
  o_ref[...] = _mlp_core(x1_scr[slot], mod_ref, g_ref, w1_ref, w2_ref, fg_ref,
                         h_scr, acc_scr, tf, side_work=next_stages)


def _pool_mlp_layer(x, mod, layer, g_mix, pool_w, pool_scale, g, w1, w2,
                    final_g=None):
  b, s, d = x.shape
  tm = TOKEN_TILE
  d_ff = w1.shape[-1]
  assert POOL_HALO >= max(POOL_WINDOWS) and tm % POOL_HALO == 0 and s % tm == 0
  tiles_per_seq = s // tm
  n_tiles = b * tiles_per_seq
  halo_per_tile = tm // POOL_HALO
  final = final_g is not None
  xf = x.reshape(b * s, d)

  def nxt(t):
    return jnp.minimum(t + 1, n_tiles - 1)

  in_specs = [
      pl.BlockSpec((tm, d), lambda t: (0, 0), pipeline_mode=pl.Buffered(1)),
      pl.BlockSpec((tm, d), lambda t: (nxt(t), 0)),
      pl.BlockSpec((POOL_HALO, d),
                   lambda t: (jnp.maximum(nxt(t) * halo_per_tile - 1, 0), 0)),
      pl.BlockSpec((None, None, 6, d),
                   lambda t: (layer, t // tiles_per_seq, 0, 0)),
      pl.BlockSpec((None, None, 6, d),
                   lambda t: (layer, nxt(t) // tiles_per_seq, 0, 0)),
      _const_spec((1, d)),
      _const_spec(pool_w.shape),
      _const_spec((1, d)),
      _const_spec((1, d)),
      _layer_spec((d, d_ff), layer),
      _layer_spec((d_ff, d), layer),
  ]
  args = [xf, xf, xf, mod, mod, g_mix.reshape(1, d), pool_w.astype(BF16),
          pool_scale.reshape(1, d), g.reshape(1, d), w1, w2]
  if final:
    in_specs.append(_const_spec((1, d)))
    args.append(final_g.reshape(1, d))
  out = pl.pallas_call(
      functools.partial(_pool_mlp_kernel, tm=tm, tf=MLP_FF_TILE,
                        tiles_per_seq=tiles_per_seq, final=final),
      grid=(n_tiles,),
      in_specs=in_specs,
      out_specs=pl.BlockSpec((tm, d), lambda t: (t, 0)),
      out_shape=jax.ShapeDtypeStruct(xf.shape, F32),
      scratch_shapes=[pltpu.VMEM((2, tm, d), F32),
                      pltpu.VMEM((POOL_HALO + tm, d), F32),
                      pltpu.VMEM((tm, d), BF16), pltpu.VMEM((tm, d), F32)],
      compiler_params=_params("arbitrary"),
      name=f"pool_mlp_{layer}",
  )(*args)
  return out.reshape(b, s, d)


def kernel(x, c, positions, ada_w, ada_b, norm_mix_g, norm_mlp_g, pool_w, pool_scale, sgu_w_in, sgu_ln_g, sgu_ln_b, sgu_w_s, sgu_b_s, sgu_w_out, mla_w_dq_dkv, mla_q_norm_g, mla_kv_norm_g, mla_w_uq, mla_w_ukv, mla_w_o, mlp_w1, mlp_w2, final_g):
  depth = ada_w.shape[0]
  mod = _ada_mod(c, ada_w, ada_b)
  w1, w2 = mlp_w1.astype(BF16), mlp_w2.astype(BF16)
  for i in range(depth):
    kind, j = i % N_MIXERS, i // N_MIXERS
    fg = final_g if i == depth - 1 else None
    if kind == 0:
      x = _pool_mlp_layer(x, mod, i, norm_mix_g[i], pool_w[j], pool_scale[j],
                          norm_mlp_g[i], w1, w2, final_g=fg)
      continue
    attn = None
    if kind == 1:
      x = _sgu_layer(x, mod, i, norm_mix_g[i], sgu_w_in[j], sgu_ln_g[j],
                     sgu_ln_b[j], sgu_w_s[j], sgu_b_s[j], sgu_w_out[j])
    else:
      w_q, w_ovo = _mla_fold(mla_w_uq[j], mla_w_ukv[j], mla_w_o[j])
      q, kv = _mla_proj(x, mod, i, norm_mix_g[i], positions, mla_w_dq_dkv[j],
                        mla_q_norm_g[j], mla_kv_norm_g[j], w_q)
      attn = (_mla_attention(q, kv), w_ovo)
    x = _mlp_layer(x, mod, i, norm_mlp_g[i], w1, w2, attn=attn, final_g=fg)
  return x
```

```python
import functools

import jax
import jax.numpy as jnp
from jax import lax
from jax.experimental import pallas as pl
from jax.experimental.pallas import tpu as pltpu

F32 = jnp.float32
BF16 = jnp.bfloat16

POOL_WINDOWS = (2, 4, 8, 16)
SGU_CHUNK = 128
SGU_HEAD_DIM = 128
MLA_HEADS = 16
MLA_Q_LORA = 256
MLA_KV_LORA = 128
MLA_NOPE = 128
MLA_ROPE = 64
MLA_V = 128
ROPE_THETA = 10000.0
RMS_EPS = 1e-6
LN_EPS = 1e-5
N_MIXERS = 3
MASK_VALUE = -1e30
LOG2_E = 1.4426950408889634
Q_SCALE = (MLA_NOPE + MLA_ROPE) ** -0.5 * LOG2_E

LANES = 128
SUBLANES = 8
VMEM_LIMIT_BYTES = 56 * 1024 * 1024

TOKEN_TILE = 512
MLP_TOKEN_TILE = 1024
SGU_TOKEN_TILE = 1024
PROJ_TOKEN_TILE = 1024
POOL_HALO = 16
ADA_COL_TILE = 2048
MLP_FF_TILE = 1024
ATTN_TILE = 512
ATTN_HEAD_GROUP = 1
ATTN_SCORES_AHEAD = 1
ATTN_VALUES_LAG = 0
ATTN_WIDE_BLOCKS = 2
QK_WIDTH = 2 * LANES


def _params(*semantics):
  return pltpu.CompilerParams(dimension_semantics=semantics,
                              vmem_limit_bytes=VMEM_LIMIT_BYTES)


def _const_spec(shape):
  zeros = (0,) * len(shape)
  return pl.BlockSpec(shape, lambda *_: zeros, pipeline_mode=pl.Buffered(1))


def _rms(x, eps):
  return x * lax.rsqrt(jnp.mean(x * x, axis=-1, keepdims=True) + eps)


def _norm_mod(x, g, shift, scale):
  return _rms(x, RMS_EPS) * g * (1.0 + scale) + shift


def _gelu(x):
  return 0.5 * x * (1.0 + lax.erf(x * (2.0 ** -0.5)))


def _ada_kernel(c_ref, w_ref, b_ref, o_ref):
  c = c_ref[...]
  ca = (c * jax.nn.sigmoid(c)).astype(BF16)
  o_ref[...] = jnp.dot(ca, w_ref[...].astype(BF16),
                       preferred_element_type=F32) + b_ref[...]


def _ada_mod(c, ada_w, ada_b):
  depth, d, d6 = ada_w.shape
  b = c.shape[0]
  tn = ADA_COL_TILE
  out = pl.pallas_call(
      _ada_kernel,
      grid=(depth, d6 // tn),
      in_specs=[
          pl.BlockSpec((b, d), lambda l, j: (0, 0)),
          pl.BlockSpec((None, d, tn), lambda l, j: (l, 0, j)),
          pl.BlockSpec((None, 1, tn), lambda l, j: (l, 0, j)),
      ],
      out_specs=pl.BlockSpec((None, b, tn), lambda l, j: (l, 0, j)),
      out_shape=jax.ShapeDtypeStruct((depth, b, d6), F32),
      compiler_params=_params("arbitrary", "arbitrary"),
      name="ada_mod",
  )(c, ada_w, ada_b.reshape(depth, 1, d6))
  return out.reshape(depth, b, 6, d)


def _pool_stages(x_ref, xh, i, g_ref, mod_ref, w_ref, ps_ref, a_scr, out_ref,
                 tm):
  def prep():
    g, shift, scale = g_ref[...], mod_ref[0:1, :], mod_ref[1:2, :]
    hh = _norm_mod(xh(), g, shift, scale)
    a_scr[0:POOL_HALO, :] = jnp.where(i > 0, hh, 0.0)
    a_scr[POOL_HALO:, :] = _norm_mod(x_ref[...], g, shift, scale)

  def group(gi, win):
    gdim = w_ref.shape[1]
    cols = slice(gi * gdim, (gi + 1) * gdim)
    s = a_scr[:, cols]
    k = 1
    while k < win:
      s = s + pltpu.roll(s, k, axis=0)
      k *= 2
    t = (i * tm + lax.broadcasted_iota(jnp.int32, (tm, 1), 0)).astype(F32)
    inv_count = 1.0 / jnp.minimum(t + 1.0, float(win))
    pooled = s[POOL_HALO:] * inv_count - a_scr[POOL_HALO:, cols]
    y = jnp.dot(pooled.astype(BF16), w_ref[gi], preferred_element_type=F32)
    out_ref[:, cols] = x_ref[:, cols] + mod_ref[2:3, cols] * (
        y * ps_ref[:, cols])

  return [prep] + [functools.partial(group, gi, win)
                   for gi, win in enumerate(POOL_WINDOWS)]


def _staggered(stages_a, stages_b):
  n = len(stages_a)
  for k in range(n + 1):
    if k < n:
      stages_a[k]()
    if k >= 1:
      stages_b[k - 1]()


def _sgu_stages(rows, n_rows, x_ref, mod_ref, g_ref, win_ref, lng_ref, lnb_ref,
                ws_ref, bst_ref, wout_ref, o_ref, gated_scr):
  st = {}
  width = wout_ref.shape[0]
  t = SGU_CHUNK

  def project_v():
    st["h"] = _norm_mod(x_ref[rows, :], g_ref[...], mod_ref[0:1, :],
                        mod_ref[1:2, :]).astype(BF16)
    st["v"] = jnp.dot(st["h"], win_ref[:, width:], preferred_element_type=F32)

  def project_u():
    st["u"] = _gelu(jnp.dot(st["h"], win_ref[:, :width],
                            preferred_element_type=F32))

  def normalise():
    v = _gelu(st["v"])
    mu = jnp.mean(v, axis=-1, keepdims=True)
    vc = v - mu
    var = jnp.mean(vc * vc, axis=-1, keepdims=True)
    st["v"] = (vc * lax.rsqrt(var + LN_EPS) * lng_ref[...]
               + lnb_ref[...]).astype(BF16)

  def gate_spatial():
    u, v = st["u"], st["v"]
    causal = (lax.broadcasted_iota(jnp.int32, (t, t), 1)
              <= lax.broadcasted_iota(jnp.int32, (t, t), 0))
    for hd in range(ws_ref.shape[0]):
      cols = slice(hd * SGU_HEAD_DIM, (hd + 1) * SGU_HEAD_DIM)
      ws = jnp.where(causal, ws_ref[hd], 0.0).astype(BF16)
      bias = bst_ref[:, hd:hd + 1]
      for c in range(n_rows // t):
        blk = slice(c * t, (c + 1) * t)
        mixed = jnp.dot(ws, v[blk, cols], preferred_element_type=F32) + bias
        gated_scr[rows.start + c * t:rows.start + (c + 1) * t, cols] = (
            u[blk, cols] * mixed).astype(BF16)

  def project_out():
    y = jnp.dot(gated_scr[rows, :], wout_ref[...], preferred_element_type=F32)
    o_ref[rows, :] = x_ref[rows, :] + mod_ref[2:3, :] * y

  return [project_v, project_u, normalise, gate_spatial, project_out]


def _sgu_kernel(*refs, tm):
  half = tm // 2
  _staggered(_sgu_stages(slice(0, half), half, *refs),
             _sgu_stages(slice(half, tm), half, *refs))


def _sgu_layer(x, mod, layer, g, w_in, ln_g, ln_b, w_s, b_s, w_out):
  b, s, d = x.shape
  tm = SGU_TOKEN_TILE
  width = w_out.shape[0]
  heads, t, _ = w_s.shape
  assert t == SGU_CHUNK and tm % t == 0 and heads * SGU_HEAD_DIM == width
  return pl.pallas_call(
      functools.partial(_sgu_kernel, tm=tm),
      grid=(b, s // tm),
      in_specs=[
          pl.BlockSpec((None, tm, d), lambda bi, i: (bi, i, 0)),
          pl.BlockSpec((None, None, 6, d), lambda bi, i: (layer, bi, 0, 0)),
          _const_spec((1, d)),
          _const_spec((d, 2 * width)),
          _const_spec((1, width)),
          _const_spec((1, width)),
          _const_spec((heads, t, t)),
          _const_spec((t, heads)),
          _const_spec((width, d)),
      ],
      out_specs=pl.BlockSpec((None, tm, d), lambda bi, i: (bi, i, 0)),
      out_shape=jax.ShapeDtypeStruct(x.shape, F32),
      scratch_shapes=[pltpu.VMEM((tm, width), BF16)],
      compiler_params=_params("arbitrary", "arbitrary"),
      name=f"sgu_mixer_{layer}",
  )(x, mod, g.reshape(1, d), w_in.astype(BF16), ln_g.reshape(1, width),
    ln_b.reshape(1, width), w_s, b_s.T, w_out.astype(BF16))


def _mla_fold_kernel(wqn_ref, wuk_ref, wuv_ref, wo_ref, qabs_ref, ovo_ref):
  qabs = lax.dot_general(
      wqn_ref[...], wuk_ref[...], (((1,), (1,)), ((), ())),
      precision=lax.Precision.HIGHEST, preferred_element_type=F32)
  qabs_ref[...] = (qabs * Q_SCALE).astype(BF16)
  ovo_ref[...] = jnp.dot(wuv_ref[...], wo_ref[...],
                         precision=lax.Precision.HIGHEST,
                         preferred_element_type=F32).astype(BF16)


def _mla_fold(w_uq, w_ukv, w_o):
  h, ql, kvl = MLA_HEADS, MLA_Q_LORA, MLA_KV_LORA
  d = w_o.shape[1]
  w_uq3 = w_uq.reshape(ql, h, MLA_NOPE + MLA_ROPE)
  w_ukv3 = w_ukv.reshape(kvl, h, MLA_NOPE + MLA_V)
  wqn = w_uq3[:, :, :MLA_NOPE].transpose(1, 0, 2)
  wuk = w_ukv3[:, :, :MLA_NOPE].transpose(1, 0, 2)
  wuv = w_ukv3[:, :, MLA_NOPE:].transpose(1, 0, 2)
  wo = w_o.reshape(h, MLA_V, d)
  qabs, ovo = pl.pallas_call(
      _mla_fold_kernel,
      grid=(h,),
      in_specs=[
          pl.BlockSpec((None, ql, MLA_NOPE), lambda i: (i, 0, 0)),
          pl.BlockSpec((None, kvl, MLA_NOPE), lambda i: (i, 0, 0)),
          pl.BlockSpec((None, kvl, MLA_V), lambda i: (i, 0, 0)),
          pl.BlockSpec((None, MLA_V, d), lambda i: (i, 0, 0)),
      ],
      out_specs=[
          pl.BlockSpec((None, ql, kvl), lambda i: (i, 0, 0)),
          pl.BlockSpec((kvl, d), lambda i: (i, 0)),
      ],
      out_shape=[
          jax.ShapeDtypeStruct((h, ql, kvl), BF16),
          jax.ShapeDtypeStruct((h * kvl, d), BF16),
      ],
      compiler_params=_params("arbitrary"),
      name="mla_fold",
  )(wqn, wuk, wuv, wo)
  half = MLA_ROPE // 2
  wr = w_uq3[:, :, MLA_NOPE:] * Q_SCALE
  zeros = jnp.zeros((ql, h, LANES - MLA_ROPE), F32)
  wra = jnp.concatenate([wr, zeros], axis=-1)
  wrb = jnp.concatenate([wr[:, :, half:], wr[:, :, :half], zeros], axis=-1)
  w_q = jnp.concatenate([
      qabs.transpose(1, 0, 2).reshape(ql, h * kvl),
      wra.reshape(ql, h * LANES).astype(BF16),
      wrb.reshape(ql, h * LANES).astype(BF16),
  ], axis=1)
  return w_q, ovo


def _mla_proj_stages(rows, x_ref, mod_ref, g_ref, pos_ref, wlat_ref, qg_ref,
                     kvg_ref, wq_ref, freq_ref, sign_ref, q_ref, kv_ref):
  st = {}
  ql, kvl = MLA_Q_LORA, MLA_KV_LORA
  n = rows.stop - rows.start
  mid = rows.start + n // 2
  heads = q_ref.shape[0]
  hw = heads * LANES

  def project_latent():
    h = _norm_mod(x_ref[rows, :], g_ref[...], mod_ref[0:1, :],
                  mod_ref[1:2, :]).astype(BF16)
    st["lat"] = jnp.dot(h, wlat_ref[...], preferred_element_type=F32)

  def rotary_and_keys():
    lat = st["lat"]
    st["cq"] = (_rms(lat[:, :ql], RMS_EPS) * qg_ref[...]).astype(BF16)
    ckv = _rms(lat[:, ql:ql + kvl], RMS_EPS) * kvg_ref[...]
    lane = lax.broadcasted_iota(jnp.int32, (1, LANES), 1)
    pos = jnp.where(lane < LANES // 2, pos_ref[rows.start:mid, :],
                    pos_ref[mid:rows.stop, :])
    ang = pos.astype(F32) * freq_ref[...]
    cos_p, sin_p = jnp.cos(ang), jnp.sin(ang)
    cos = jnp.concatenate([cos_p, pltpu.roll(cos_p, LANES // 2, axis=1)], axis=0)
    sin = jnp.concatenate([sin_p, pltpu.roll(sin_p, LANES // 2, axis=1)], axis=0)
    sin = sin * sign_ref[...]
    kra = lat[:, ql + kvl:ql + kvl + LANES]
    krb = lat[:, ql + kvl + LANES:]
    kv_ref[rows, :LANES] = ckv.astype(BF16)
    kv_ref[rows, LANES:] = (kra * cos + krb * sin).astype(BF16)
    st["cos2"] = jnp.concatenate([cos, cos], axis=1)
    st["sin2"] = jnp.concatenate([sin, sin], axis=1)

  def queries(pairs):
    cq = st["cq"]
    for p in pairs:
      c0 = 2 * p * LANES
      qa = jnp.dot(cq, wq_ref[:, c0:c0 + 2 * LANES],
                   preferred_element_type=F32)
      ra = jnp.dot(cq, wq_ref[:, hw + c0:hw + c0 + 2 * LANES],
                   preferred_element_type=F32)
      rb = jnp.dot(cq, wq_ref[:, 2 * hw + c0:2 * hw + c0 + 2 * LANES],
                   preferred_element_type=F32)
      qr = ra * st["cos2"] + rb * st["sin2"]
      for k in range(2):
        lanes = slice(k * LANES, (k + 1) * LANES)
        q_ref[2 * p + k, rows, :LANES] = qa[:, lanes].astype(BF16)
        q_ref[2 * p + k, rows, LANES:] = qr[:, lanes].astype(BF16)

  n_pairs = heads // 2
  return [project_latent, rotary_and_keys,
          functools.partial(queries, range(0, n_pairs // 2)),
          functools.partial(queries, range(n_pairs // 2, n_pairs))]


def _mla_proj_kernel(*refs, tm):
  half = tm // 2
  _staggered(_mla_proj_stages(slice(0, half), *refs),
             _mla_proj_stages(slice(half, tm), *refs))


def _mla_proj(x, mod, layer, g, positions, w_dq_dkv, q_norm_g, kv_norm_g, w_q):
  b, s, d = x.shape
  tm = PROJ_TOKEN_TILE
  ql, kvl, rope, h = MLA_Q_LORA, MLA_KV_LORA, MLA_ROPE, MLA_HEADS
  half = rope // 2
  assert 2 * rope == LANES
  wkr = w_dq_dkv[:, ql + kvl:]
  zeros = jnp.zeros((d, LANES - rope), F32)
  w_lat = jnp.concatenate(
      [w_dq_dkv[:, :ql + kvl], wkr, zeros, wkr[:, half:], wkr[:, :half], zeros],
      axis=1).astype(BF16)
  inv_freq = ROPE_THETA ** (-jnp.arange(0, rope, 2, dtype=F32) / rope)
  freq = jnp.tile(inv_freq, LANES // half).reshape(1, LANES)
  sign = jnp.concatenate([-jnp.ones((half,), F32),
                          jnp.ones((LANES - half,), F32)]).reshape(1, LANES)
  return pl.pallas_call(
      functools.partial(_mla_proj_kernel, tm=tm),
      grid=(b, s // tm),
      in_specs=[
          pl.BlockSpec((None, tm, d), lambda bi, i: (bi, i, 0)),
          pl.BlockSpec((None, None, 6, d), lambda bi, i: (layer, bi, 0, 0)),
          _const_spec((1, d)),
          pl.BlockSpec((None, tm, 1), lambda bi, i: (bi, i, 0)),
          _const_spec(w_lat.shape),
          _const_spec((1, ql)),
          _const_spec((1, kvl)),
          _const_spec(w_q.shape),
          _const_spec((1, LANES)),
          _const_spec((1, LANES)),
      ],
      out_specs=[
          pl.BlockSpec((None, h, tm, QK_WIDTH), lambda bi, i: (bi, 0, i, 0)),
          pl.BlockSpec((None, tm, QK_WIDTH), lambda bi, i: (bi, i, 0)),
      ],
      out_shape=[
          jax.ShapeDtypeStruct((b, h, s, QK_WIDTH), BF16),
          jax.ShapeDtypeStruct((b, s, QK_WIDTH), BF16),
      ],
      compiler_params=_params("arbitrary", "arbitrary"),
      name=f"mla_proj_{layer}",
  )(x, mod, g.reshape(1, d), positions.reshape(b, s, 1), w_lat,
    q_norm_g.reshape(1, ql), kv_norm_g.reshape(1, kvl), w_q, freq, sign)


def _flash_kernel(q_ref, kv_ref, o_ref, m_scr, l_scr, acc_scr, *, tq, group):
  i = pl.program_id(1)
  heads = q_ref.shape[0]
  rows_per_group = group * tq
  n_groups = heads // group

  def block(key_start, n_keys, diagonal):
    kv = kv_ref[pl.ds(key_start, n_keys), :]
    v_ones = jnp.concatenate([kv[:, :LANES], jnp.ones((n_keys, LANES), BF16)],
                             axis=1)
    s, p, alpha = {}, {}, {}
    if diagonal:
      visible = (lax.broadcasted_iota(jnp.int32, (tq, n_keys), 1)
                 <= lax.broadcasted_iota(jnp.int32, (tq, n_keys), 0))
      visible = jnp.concatenate([visible] * group, axis=0)

    def rows_of(gi):
      return slice(gi * rows_per_group, (gi + 1) * rows_per_group)

    def scores(gi):
      q = q_ref[gi * group:(gi + 1) * group].reshape(rows_per_group, QK_WIDTH)
      s[gi] = lax.dot_general(q, kv, (((1,), (1,)), ((), ())),
                              preferred_element_type=F32)

    def softmax(gi):
      rows = rows_of(gi)
      sg = s.pop(gi)
      if diagonal:
        sg = jnp.where(visible, sg, MASK_VALUE)
        m_new = jnp.broadcast_to(jnp.max(sg, axis=1, keepdims=True),
                                 (rows_per_group, LANES))
      else:
        m_prev = m_scr[rows]
        m_new = jnp.maximum(m_prev, jnp.max(sg, axis=1, keepdims=True))
        alpha[gi] = jnp.exp2(m_prev - m_new)
      ps = [jnp.exp2(sg[:, k * LANES:(k + 1) * LANES] - m_new)
            for k in range(n_keys // LANES)]
      p[gi] = jnp.concatenate(ps, axis=1).astype(BF16)
      m_scr[rows] = m_new

    def weighted_values(gi):
      rows = rows_of(gi)
      pv = jnp.dot(p.pop(gi), v_ones, preferred_element_type=F32)
      if diagonal:
        acc_scr[rows] = pv[:, :LANES]
        l_scr[rows] = pv[:, LANES:]
      else:
        a = alpha.pop(gi)
        acc_scr[rows] = a * acc_scr[rows] + pv[:, :LANES]
        l_scr[rows] = a * l_scr[rows] + pv[:, LANES:]

    ahead, lag = ATTN_SCORES_AHEAD, ATTN_VALUES_LAG
    for gi in range(min(ahead, n_groups)):
      scores(gi)
    for gi in range(n_groups + lag):
      if gi + ahead < n_groups:
        scores(gi + ahead)
      if gi < n_groups:
        softmax(gi)
      if 0 <= gi - lag < n_groups:
        weighted_values(gi - lag)

  block(pl.multiple_of(i * tq, tq), tq, True)
  wide = ATTN_WIDE_BLOCKS * tq

  def wide_block(j, carry):
    block(pl.multiple_of(j * wide, wide), wide, False)
    return carry

  n_wide = i // ATTN_WIDE_BLOCKS
  lax.fori_loop(0, n_wide, wide_block, 0)
  for r in range(ATTN_WIDE_BLOCKS - 1):
    @pl.when(i - n_wide * ATTN_WIDE_BLOCKS > r)
    def _(r=r):
      block(pl.multiple_of((n_wide * ATTN_WIDE_BLOCKS + r) * tq, tq), tq, False)
  for hd in range(heads):
    rows = slice(hd * tq, (hd + 1) * tq)
    o_ref[:, hd * LANES:(hd + 1) * LANES] = (
        acc_scr[rows] / l_scr[rows]).astype(BF16)


def _mla_attention(q, kv):
  b, h, s, w = q.shape
  tq = ATTN_TILE
  rows = h * tq
  return pl.pallas_call(
      functools.partial(_flash_kernel, tq=tq, group=ATTN_HEAD_GROUP),
      grid=(b, s // tq),
      in_specs=[
          pl.BlockSpec((None, h, tq, w), lambda bi, i: (bi, 0, i, 0)),
          pl.BlockSpec((None, s, w), lambda bi, i: (bi, 0, 0)),
      ],
      out_specs=pl.BlockSpec((None, tq, h * LANES), lambda bi, i: (bi, i, 0)),
      out_shape=jax.ShapeDtypeStruct((b, s, h * LANES), BF16),
      scratch_shapes=[pltpu.VMEM((rows, LANES), F32)] * 3,
      compiler_params=_params("arbitrary", "arbitrary"),
      name="mla_attention",
  )(q, kv)


def _mlp_core(x, mod_ref, g_ref, w1_ref, w2_ref, fg_ref, h_scr, acc_scr, tf,
              side_work=()):
  h_scr[...] = _norm_mod(x, g_ref[...], mod_ref[3:4, :],
                         mod_ref[4:5, :]).astype(BF16)
  d_ff = w1_ref.shape[1]
  n_chunks = d_ff // tf
  side_work = list(side_work)
  for j in range(n_chunks):
    cols = slice(j * tf, (j + 1) * tf)
    a = jnp.dot(h_scr[...], w1_ref[:, cols], preferred_element_type=F32)
    a = jnp.maximum(a, 0.0)
    part = jnp.dot((a * a).astype(BF16), w2_ref[cols, :],
                   preferred_element_type=F32)
    if j == 0:
      acc_scr[...] = part
    else:
      acc_scr[...] += part
    slots = max(n_chunks - 1, 1)
    lo = min(-(-j * len(side_work) // slots), len(side_work))
    hi = min(-(-(j + 1) * len(side_work) // slots), len(side_work))
    for work in side_work[lo:hi]:
      work()
  out = x + mod_ref[5:6, :] * acc_scr[...]
  if fg_ref is not None:
    out = _rms(out, RMS_EPS) * fg_ref[...]
  return out


def _mlp_stages(rows, get_x, mod_ref, g_ref, w1_ref, w2_ref, fg_ref, o_ref,
                h_scr, acc_scr, tf):
  def normalise():
    x = get_x()
    o_ref[rows, :] = x
    h_scr[rows, :] = _norm_mod(x, g_ref[...], mod_ref[3:4, :],
                               mod_ref[4:5, :]).astype(BF16)

  def chunk(j):
    cols = slice(j * tf, (j + 1) * tf)
    a = jnp.dot(h_scr[rows, :], w1_ref[:, cols], preferred_element_type=F32)
    a = jnp.maximum(a, 0.0)
    part = jnp.dot((a * a).astype(BF16), w2_ref[cols, :],
                   preferred_element_type=F32)
    if j == 0:
      acc_scr[rows, :] = part
    else:
      acc_scr[rows, :] += part

  def finish():
    out = o_ref[rows, :] + mod_ref[5:6, :] * acc_scr[rows, :]
    if fg_ref is not None:
      out = _rms(out, RMS_EPS) * fg_ref[...]
    o_ref[rows, :] = out

  n_chunks = w1_ref.shape[1] // tf
  return ([normalise] + [functools.partial(chunk, j) for j in range(n_chunks)]
          + [finish])


def _mlp_kernel(*refs, tm, tf, with_attn, final):
  refs = list(refs)
  x_ref, mod_ref, g_ref, w1_ref, w2_ref = refs[:5]
  rest = refs[5:]
  if with_attn:
    attn_ref, wo_ref = rest[:2]
    rest = rest[2:]
  fg_ref = None
  if final:
    fg_ref = rest[0]
    rest = rest[1:]
  o_ref, h_scr, acc_scr = rest

  def stages(rows):
    def get_x():
      x = x_ref[rows, :]
      if with_attn:
        y = jnp.dot(attn_ref[rows, :], wo_ref[...], preferred_element_type=F32)
        x = x + mod_ref[2:3, :] * y
      return x
    return _mlp_stages(rows, get_x, mod_ref, g_ref, w1_ref, w2_ref, fg_ref,
                       o_ref, h_scr, acc_scr, tf)

  half = tm // 2
  _staggered(stages(slice(0, half)), stages(slice(half, tm)))


def _layer_spec(shape, layer):
  zeros = (0,) * len(shape)
  return pl.BlockSpec((None,) + tuple(shape), lambda *_: (layer,) + zeros,
                      pipeline_mode=pl.Buffered(1))


def _mlp_layer(x, mod, layer, g, w1, w2, attn=None, final_g=None):
  b, s, d = x.shape
  tm = MLP_TOKEN_TILE
  d_ff = w1.shape[-1]
  with_attn = attn is not None
  final = final_g is not None
  in_specs = [
      pl.BlockSpec((None, tm, d), lambda bi, i: (bi, i, 0)),
      pl.BlockSpec((None, None, 6, d), lambda bi, i: (layer, bi, 0, 0)),
      _const_spec((1, d)),
      _layer_spec((d, d_ff), layer),
      _layer_spec((d_ff, d), layer),
  ]
  args = [x, mod, g.reshape(1, d), w1, w2]
  if with_attn:
    attn_out, w_ovo = attn
    in_specs += [
        pl.BlockSpec((None, tm, attn_out.shape[-1]), lambda bi, i: (bi, i, 0)),
        _const_spec(w_ovo.shape),
    ]
    args += [attn_out, w_ovo]
  if final:
    in_specs.append(_const_spec((1, d)))
    args.append(final_g.reshape(1, d))
  return pl.pallas_call(
      functools.partial(_mlp_kernel, tm=tm, tf=MLP_FF_TILE,
                        with_attn=with_attn, final=final),
      grid=(b, s // tm),
      in_specs=in_specs,
      out_specs=pl.BlockSpec((None, tm, d), lambda bi, i: (bi, i, 0)),
      out_shape=jax.ShapeDtypeStruct(x.shape, F32),
      scratch_shapes=[pltpu.VMEM((tm, d), BF16), pltpu.VMEM((tm, d), F32)],
      compiler_params=_params("arbitrary", "arbitrary"),
      name=f"mlp_{layer}",
  )(*args)


def _pool_mlp_kernel(*refs, tm, tf, tiles_per_seq, final):
  refs = list(refs)
  (x0_ref, xn_ref, xhn_ref, mod_ref, modn_ref, gmix_ref, pw_ref, ps_ref,
   g_ref, w1_ref, w2_ref) = refs[:11]
  rest = refs[11:]
  fg_ref = None
  if final:
    fg_ref = rest[0]
    rest = rest[1:]
  o_ref, x1_scr, a_scr, h_scr, acc_scr = rest
  t = pl.program_id(0)
  slot = t % 2

  @pl.when(t == 0)
  def _():
    for stage in _pool_stages(x0_ref, lambda: x0_ref[0:POOL_HALO, :], 0,
                              gmix_ref, mod_ref, pw_ref, ps_ref, a_scr,
                              x1_scr.at[0], tm):
      stage()

  next_stages = _pool_stages(xn_ref, lambda: xhn_ref[...],
                             (t + 1) % tiles_per_seq, gmix_ref, modn_ref,
                             pw_ref, ps_ref, a_scr, x1_scr.at[1 - slot], tm)
  o_ref[...] = _mlp_core(x1_scr[slot], mod_ref, g_ref, w1_ref, w2_ref, fg_ref,
                         h_scr, acc_scr, tf, side_work=next_stages)


def _pool_mlp_layer(x, mod, layer, g_mix, pool_w, pool_scale, g, w1, w2,
                    final_g=None):
  b, s, d = x.shape
  tm = TOKEN_TILE
  d_ff = w1.shape[-1]
  assert POOL_HALO >= max(POOL_WINDOWS) and tm % POOL_HALO == 0 and s % tm == 0
  tiles_per_seq = s // tm
  n_tiles = b * tiles_per_seq
  halo_per_tile = tm // POOL_HALO
  final = final_g is not None
  xf = x.reshape(b * s, d)

  def nxt(t):
    return jnp.minimum(t + 1, n_tiles - 1)

  in_specs = [
      pl.BlockSpec((tm, d), lambda t: (0, 0), pipeline_mode=pl.Buffered(1)),
      pl.BlockSpec((tm, d), lambda t: (nxt(t), 0)),
      pl.BlockSpec((POOL_HALO, d),
                   lambda t: (jnp.maximum(nxt(t) * halo_per_tile - 1, 0), 0)),
      pl.BlockSpec((None, None, 6, d),
                   lambda t: (layer, t // tiles_per_seq, 0, 0)),
      pl.BlockSpec((None, None, 6, d),
                   lambda t: (layer, nxt(t) // tiles_per_seq, 0, 0)),
      _const_spec((1, d)),
      _const_spec(pool_w.shape),
      _const_spec((1, d)),
      _const_spec((1, d)),
      _layer_spec((d, d_ff), layer),
      _layer_spec((d_ff, d), layer),
  ]
  args = [xf, xf, xf, mod, mod, g_mix.reshape(1, d), pool_w.astype(BF16),
          pool_scale.reshape(1, d), g.reshape(1, d), w1, w2]
  if final:
    in_specs.append(_const_spec((1, d)))
    args.append(final_g.reshape(1, d))
  out = pl.pallas_call(
      functools.partial(_pool_mlp_kernel, tm=tm, tf=MLP_FF_TILE,
                        tiles_per_seq=tiles_per_seq, final=final),
      grid=(n_tiles,),
      in_specs=in_specs,
      out_specs=pl.BlockSpec((tm, d), lambda t: (t, 0)),
      out_shape=jax.ShapeDtypeStruct(xf.shape, F32),
      scratch_shapes=[pltpu.VMEM((2, tm, d), F32),
                      pltpu.VMEM((POOL_HALO + tm, d), F32),
                      pltpu.VMEM((tm, d), BF16), pltpu.VMEM((tm, d), F32)],
      compiler_params=_params("arbitrary"),
      name=f"pool_mlp_{layer}",
  )(*args)
  return out.reshape(b, s, d)


def kernel(x, c, positions, ada_w, ada_b, norm_mix_g, norm_mlp_g, pool_w, pool_scale, sgu_w_in, sgu_ln_g, sgu_ln_b, sgu_w_s, sgu_b_s, sgu_w_out, mla_w_dq_dkv, mla_q_norm_g, mla_kv_norm_g, mla_w_uq, mla_w_ukv, mla_w_o, mlp_w1, mlp_w2, final_g):
  depth = ada_w.shape[0]
  mod = _ada_mod(c, ada_w, ada_b)
  w1, w2 = mlp_w1.astype(BF16), mlp_w2.astype(BF16)
  for i in range(depth):
    kind, j = i % N_MIXERS, i // N_MIXERS
    fg = final_g if i == depth - 1 else None
    if kind == 0:
      x = _pool_mlp_layer(x, mod, i, norm_mix_g[i], pool_w[j], pool_scale[j],
                          norm_mlp_g[i], w1, w2, final_g=fg)
      continue
    attn = None
    if kind == 1:
      x = _sgu_layer(x, mod, i, norm_mix_g[i], sgu_w_in[j], sgu_ln_g[j],
                     sgu_ln_b[j], sgu_w_s[j], sgu_b_s[j], sgu_w_out[j])
    else:
      w_q, w_ovo = _mla_fold(mla_w_uq[j], mla_w_ukv[j], mla_w_o[j])
      q, kv = _mla_proj(x, mod, i, norm_mix_g[i], positions, mla_w_dq_dkv[j],
                        mla_q_norm_g[j], mla_kv_norm_g[j], w_q)
      attn = (_mla_attention(q, kv), w_ovo)
    x = _mlp_layer(x, mod, i, norm_mlp_g[i], w1, w2, attn=attn, final_g=fg)
  return x
```

```python
import functools

import jax
import jax.numpy as jnp
from jax import lax
from jax.experimental import pallas as pl
from jax.experimental.pallas import tpu as pltpu

F32 = jnp.float32
BF16 = jnp.bfloat16

POOL_WINDOWS = (2, 4, 8, 16)
SGU_CHUNK = 128
SGU_HEAD_DIM = 128
MLA_HEADS = 16
MLA_Q_LORA = 256
MLA_KV_LORA = 128
MLA_NOPE = 128
MLA_ROPE = 64
MLA_V = 128
ROPE_THETA = 10000.0
RMS_EPS = 1e-6
LN_EPS = 1e-5
N_MIXERS = 3
MASK_VALUE = -1e30
LOG2_E = 1.4426950408889634
Q_SCALE = (MLA_NOPE + MLA_ROPE) ** -0.5 * LOG2_E

LANES = 128
VMEM_LIMIT_BYTES = 56 * 1024 * 1024

TOKEN_TILE = 512
MLP_TOKEN_TILE = 1024
SGU_TOKEN_TILE = 1024
PROJ_TOKEN_TILE = 1024
POOL_HALO = 16
ADA_COL_TILE = 2048
MLP_FF_TILE = 1024
ATTN_TILE = 512
ATTN_WIDE_BLOCKS = 2
QK_WIDTH = 2 * LANES


def _params(*semantics):
  return pltpu.CompilerParams(dimension_semantics=semantics,
                              vmem_limit_bytes=VMEM_LIMIT_BYTES)


def _const_spec(shape):
  zeros = (0,) * len(shape)
  return pl.BlockSpec(shape, lambda *_: zeros, pipeline_mode=pl.Buffered(1))


def _rms(x, eps):
  return x * lax.rsqrt(jnp.mean(x * x, axis=-1, keepdims=True) + eps)


def _norm_mod(x, g, shift, scale):
  return _rms(x, RMS_EPS) * g * (1.0 + scale) + shift


def _gelu(x):
  return 0.5 * x * (1.0 + lax.erf(x * (2.0 ** -0.5)))


def _staggered(stages_a, stages_b):
  n = len(stages_a)
  for k in range(n + 1):
    if k < n:
      stages_a[k]()
    if k >= 1:
      stages_b[k - 1]()


def _ada_kernel(c_ref, w_ref, b_ref, o_ref):
  c = c_ref[...]
  ca = (c * jax.nn.sigmoid(c)).astype(BF16)
  o_ref[...] = jnp.dot(ca, w_ref[...].astype(BF16),
                       preferred_element_type=F32) + b_ref[...]


def _ada_mod(c, ada_w, ada_b):
  depth, d, d6 = ada_w.shape
  b = c.shape[0]
  tn = ADA_COL_TILE
  out = pl.pallas_call(
      _ada_kernel,
      grid=(depth, d6 // tn),
      in_specs=[
          pl.BlockSpec((b, d), lambda l, j: (0, 0)),
          pl.BlockSpec((None, d, tn), lambda l, j: (l, 0, j)),
          pl.BlockSpec((None, 1, tn), lambda l, j: (l, 0, j)),
      ],
      out_specs=pl.BlockSpec((None, b, tn), lambda l, j: (l, 0, j)),
      out_shape=jax.ShapeDtypeStruct((depth, b, d6), F32),
      compiler_params=_params("arbitrary", "arbitrary"),
      name="ada_mod",
  )(c, ada_w, ada_b.reshape(depth, 1, d6))
  return out.reshape(depth, b, 6, d)


def _pool_stages(x_ref, xh, i, g_ref, mod_ref, w_ref, ps_ref, a_scr, out_ref,
                 tm):
  def prep():
    g, shift, scale = g_ref[...], mod_ref[0:1, :], mod_ref[1:2, :]
    hh = _norm_mod(xh(), g, shift, scale)
    a_scr[0:POOL_HALO, :] = jnp.where(i > 0, hh, 0.0)
    a_scr[POOL_HALO:, :] = _norm_mod(x_ref[...], g, shift, scale)

  def group(gi, win):
    gdim = w_ref.shape[1]
    cols = slice(gi * gdim, (gi + 1) * gdim)
    s = a_scr[:, cols]
    k = 1
    while k < win:
      s = s + pltpu.roll(s, k, axis=0)
      k *= 2
    t = (i * tm + lax.broadcasted_iota(jnp.int32, (tm, 1), 0)).astype(F32)
    inv_count = 1.0 / jnp.minimum(t + 1.0, float(win))
    pooled = s[POOL_HALO:] * inv_count - a_scr[POOL_HALO:, cols]
    y = jnp.dot(pooled.astype(BF16), w_ref[gi], preferred_element_type=F32)
    out_ref[:, cols] = x_ref[:, cols] + mod_ref[2:3, cols] * (
        y * ps_ref[:, cols])

  return [prep] + [functools.partial(group, gi, win)
                   for gi, win in enumerate(POOL_WINDOWS)]


def _sgu_stages(rows, n_rows, x_ref, mod_ref, g_ref, win_ref, lng_ref, lnb_ref,
                ws_ref, bst_ref, wout_ref, o_ref, gated_scr):
  st = {}
  width = wout_ref.shape[0]
  t = SGU_CHUNK

  def project_v():
    st["h"] = _norm_mod(x_ref[rows, :], g_ref[...], mod_ref[0:1, :],
                        mod_ref[1:2, :]).astype(BF16)
    st["v"] = jnp.dot(st["h"], win_ref[:, width:], preferred_element_type=F32)

  def project_u():
    st["u"] = _gelu(jnp.dot(st["h"], win_ref[:, :width],
                            preferred_element_type=F32))

  def normalise():
    v = _gelu(st["v"])
    mu = jnp.mean(v, axis=-1, keepdims=True)
    vc = v - mu
    var = jnp.mean(vc * vc, axis=-1, keepdims=True)
    st["v"] = (vc * lax.rsqrt(var + LN_EPS) * lng_ref[...]
               + lnb_ref[...]).astype(BF16)

  def gate_spatial():
    u, v = st["u"], st["v"]
    causal = (lax.broadcasted_iota(jnp.int32, (t, t), 1)
              <= lax.broadcasted_iota(jnp.int32, (t, t), 0))
    for hd in range(ws_ref.shape[0]):
      cols = slice(hd * SGU_HEAD_DIM, (hd + 1) * SGU_HEAD_DIM)
      ws = jnp.where(causal, ws_ref[hd], 0.0).astype(BF16)
      bias = bst_ref[:, hd:hd + 1]
      for c in range(n_rows // t):
        blk = slice(c * t, (c + 1) * t)
        mixed = jnp.dot(ws, v[blk, cols], preferred_element_type=F32) + bias
        gated_scr[rows.start + c * t:rows.start + (c + 1) * t, cols] = (
            u[blk, cols] * mixed).astype(BF16)

  def project_out():
    y = jnp.dot(gated_scr[rows, :], wout_ref[...], preferred_element_type=F32)
    o_ref[rows, :] = x_ref[rows, :] + mod_ref[2:3, :] * y

  return [project_v, project_u, normalise, gate_spatial, project_out]


def _sgu_kernel(*refs, tm):
  half = tm // 2
  _staggered(_sgu_stages(slice(0, half), half, *refs),
             _sgu_stages(slice(half, tm), half, *refs))


def _sgu_layer(x, mod, layer, g, w_in, ln_g, ln_b, w_s, b_s, w_out):
  b, s, d = x.shape
  tm = SGU_TOKEN_TILE
  width = w_out.shape[0]
  heads, t, _ = w_s.shape
  assert t == SGU_CHUNK and tm % t == 0 and heads * SGU_HEAD_DIM == width
  return pl.pallas_call(
      functools.partial(_sgu_kernel, tm=tm),
      grid=(b, s // tm),
      in_specs=[
          pl.BlockSpec((None, tm, d), lambda bi, i: (bi, i, 0)),
          pl.BlockSpec((None, None, 6, d), lambda bi, i: (layer, bi, 0, 0)),
          _const_spec((1, d)),
          _const_spec((d, 2 * width)),
          _const_spec((1, width)),
          _const_spec((1, width)),
          _const_spec((heads, t, t)),
          _const_spec((t, heads)),
          _const_spec((width, d)),
      ],
      out_specs=pl.BlockSpec((None, tm, d), lambda bi, i: (bi, i, 0)),
      out_shape=jax.ShapeDtypeStruct(x.shape, F32),
      scratch_shapes=[pltpu.VMEM((tm, width), BF16)],
      compiler_params=_params("arbitrary", "arbitrary"),
      name=f"sgu_mixer_{layer}",
  )(x, mod, g.reshape(1, d), w_in.astype(BF16), ln_g.reshape(1, width),
    ln_b.reshape(1, width), w_s, b_s.T, w_out.astype(BF16))


def _mla_fold_kernel(wqn_ref, wuk_ref, wuv_ref, wo_ref, qabs_ref, ovo_ref):
  qabs = lax.dot_general(
      wqn_ref[...], wuk_ref[...], (((1,), (1,)), ((), ())),
      precision=lax.Precision.HIGHEST, preferred_element_type=F32)
  qabs_ref[...] = (qabs * Q_SCALE).astype(BF16)
  ovo_ref[...] = jnp.dot(wuv_ref[...], wo_ref[...],
                         precision=lax.Precision.HIGHEST,
                         preferred_element_type=F32).astype(BF16)


def _mla_fold(w_uq, w_ukv, w_o):
  h, ql, kvl = MLA_HEADS, MLA_Q_LORA, MLA_KV_LORA
  d = w_o.shape[1]
  w_uq3 = w_uq.reshape(ql, h, MLA_NOPE + MLA_ROPE)
  w_ukv3 = w_ukv.reshape(kvl, h, MLA_NOPE + MLA_V)
  wqn = w_uq3[:, :, :MLA_NOPE].transpose(1, 0, 2)
  wuk = w_ukv3[:, :, :MLA_NOPE].transpose(1, 0, 2)
  wuv = w_ukv3[:, :, MLA_NOPE:].transpose(1, 0, 2)
  wo = w_o.reshape(h, MLA_V, d)
  qabs, ovo = pl.pallas_call(
      _mla_fold_kernel,
      grid=(h,),
      in_specs=[
          pl.BlockSpec((None, ql, MLA_NOPE), lambda i: (i, 0, 0)),
          pl.BlockSpec((None, kvl, MLA_NOPE), lambda i: (i, 0, 0)),
          pl.BlockSpec((None, kvl, MLA_V), lambda i: (i, 0, 0)),
          pl.BlockSpec((None, MLA_V, d), lambda i: (i, 0, 0)),
      ],
      out_specs=[
          pl.BlockSpec((None, ql, kvl), lambda i: (i, 0, 0)),
          pl.BlockSpec((kvl, d), lambda i: (i, 0)),
      ],
      out_shape=[
          jax.ShapeDtypeStruct((h, ql, kvl), BF16),
          jax.ShapeDtypeStruct((h * kvl, d), BF16),
      ],
      compiler_params=_params("arbitrary"),
      name="mla_fold",
  )(wqn, wuk, wuv, wo)
  half = MLA_ROPE // 2
  wr = w_uq3[:, :, MLA_NOPE:] * Q_SCALE
  wr_swapped = jnp.concatenate([wr[:, :, half:], wr[:, :, :half]], axis=-1)
  w_q = jnp.concatenate([
      qabs.transpose(1, 0, 2).reshape(ql, h * kvl),
      wr.reshape(ql, h * MLA_ROPE).astype(BF16),
      wr_swapped.reshape(ql, h * MLA_ROPE).astype(BF16),
  ], axis=1)
  return w_q, ovo


def _mla_proj_stages(rows, x_ref, mod_ref, g_ref, pos_ref, wlat_ref, qg_ref,
                     kvg_ref, wq_ref, freq_ref, sign_ref, q_ref, kv_ref):
  st = {}
  ql, kvl = MLA_Q_LORA, MLA_KV_LORA
  n = rows.stop - rows.start
  mid = rows.start + n // 2
  heads = q_ref.shape[0]
  hw = heads * LANES

  def project_latent():
    h = _norm_mod(x_ref[rows, :], g_ref[...], mod_ref[0:1, :],
                  mod_ref[1:2, :]).astype(BF16)
    st["lat"] = jnp.dot(h, wlat_ref[...], preferred_element_type=F32)

  def rotary_and_keys():
    lat = st["lat"]
    st["cq"] = (_rms(lat[:, :ql], RMS_EPS) * qg_ref[...]).astype(BF16)
    ckv = _rms(lat[:, ql:ql + kvl], RMS_EPS) * kvg_ref[...]
    low = lax.broadcasted_iota(jnp.int32, (1, LANES), 1) < LANES // 2
    pos = jnp.where(low, pos_ref[rows.start:mid, :], pos_ref[mid:rows.stop, :])
    ang = pos.astype(F32) * freq_ref[...]

    def spread(t):
      r = pltpu.roll(t, LANES // 2, axis=1)
      return jnp.concatenate([jnp.where(low, t, r), jnp.where(low, r, t)],
                             axis=0)

    cos = spread(jnp.cos(ang))
    sin = spread(jnp.sin(ang)) * sign_ref[...]
    kra = lat[:, ql + kvl:ql + kvl + LANES]
    krb = lat[:, ql + kvl + LANES:]
    kv_ref[rows, :LANES] = ckv.astype(BF16)
    kv_ref[rows, LANES:] = (kra * cos + krb * sin).astype(BF16)
    st["cos2"] = jnp.concatenate([cos, cos], axis=1)
    st["sin2"] = jnp.concatenate([sin, sin], axis=1)

  def queries(quads):
    cq = st["cq"]
    hr = heads * MLA_ROPE
    for u in quads:
      r0 = hw + 2 * u * LANES
      ra = jnp.dot(cq, wq_ref[:, r0:r0 + 2 * LANES],
                   preferred_element_type=F32)
      rb = jnp.dot(cq, wq_ref[:, hr + r0:hr + r0 + 2 * LANES],
                   preferred_element_type=F32)
      qr = ra * st["cos2"] + rb * st["sin2"]
      for v in range(2):
        c0 = (4 * u + 2 * v) * LANES
        qa = jnp.dot(cq, wq_ref[:, c0:c0 + 2 * LANES],
                     preferred_element_type=F32)
        pair = qr[:, v * LANES:(v + 1) * LANES]
        tiles = (pair, pltpu.roll(pair, LANES // 2, axis=1))
        for k in range(2):
          hd = 4 * u + 2 * v + k
          q_ref[hd, rows, :LANES] = qa[:, k * LANES:(k + 1) * LANES].astype(BF16)
          q_ref[hd, rows, LANES:] = tiles[k].astype(BF16)

  n_quads = heads // 4
  return [project_latent, rotary_and_keys,
          functools.partial(queries, range(0, n_quads // 2)),
          functools.partial(queries, range(n_quads // 2, n_quads))]


def _mla_proj_kernel(*refs, tm):
  half = tm // 2
  _staggered(_mla_proj_stages(slice(0, half), *refs),
             _mla_proj_stages(slice(half, tm), *refs))


def _mla_proj(x, mod, layer, g, positions, w_dq_dkv, q_norm_g, kv_norm_g, w_q):
  b, s, d = x.shape
  tm = PROJ_TOKEN_TILE
  ql, kvl, rope, h = MLA_Q_LORA, MLA_KV_LORA, MLA_ROPE, MLA_HEADS
  half = rope // 2
  assert 2 * rope == LANES and h % 4 == 0
  wkr = w_dq_dkv[:, ql + kvl:]
  zeros = jnp.zeros((d, LANES - rope), F32)
  w_lat = jnp.concatenate(
      [w_dq_dkv[:, :ql + kvl], wkr, zeros, wkr[:, half:], wkr[:, :half], zeros],
      axis=1).astype(BF16)
  inv_freq = ROPE_THETA ** (-jnp.arange(0, rope, 2, dtype=F32) / rope)
  freq = jnp.tile(inv_freq, LANES // half).reshape(1, LANES)
  sign = jnp.tile(jnp.concatenate([-jnp.ones((half,), F32),
                                   jnp.ones((half,), F32)]),
                  LANES // rope).reshape(1, LANES)
  return pl.pallas_call(
      functools.partial(_mla_proj_kernel, tm=tm),
      grid=(b, s // tm),
      in_specs=[
          pl.BlockSpec((None, tm, d), lambda bi, i: (bi, i, 0)),
          pl.BlockSpec((None, None, 6, d), lambda bi, i: (layer, bi, 0, 0)),
          _const_spec((1, d)),
          pl.BlockSpec((None, tm, 1), lambda bi, i: (bi, i, 0)),
          _const_spec(w_lat.shape),
          _const_spec((1, ql)),
          _const_spec((1, kvl)),
          _const_spec(w_q.shape),
          _const_spec((1, LANES)),
          _const_spec((1, LANES)),
      ],
      out_specs=[
          pl.BlockSpec((None, h, tm, QK_WIDTH), lambda bi, i: (bi, 0, i, 0)),
          pl.BlockSpec((None, tm, QK_WIDTH), lambda bi, i: (bi, i, 0)),
      ],
      out_shape=[
          jax.ShapeDtypeStruct((b, h, s, QK_WIDTH), BF16),
          jax.ShapeDtypeStruct((b, s, QK_WIDTH), BF16),
      ],
      compiler_params=_params("arbitrary", "arbitrary"),
      name=f"mla_proj_{layer}",
  )(x, mod, g.reshape(1, d), positions.reshape(b, s, 1), w_lat,
    q_norm_g.reshape(1, ql), kv_norm_g.reshape(1, kvl), w_q, freq, sign)


def _flash_kernel(q_ref, kv_ref, o_ref, m_scr, l_scr, acc_scr, *, tq):
  i = pl.program_id(1)
  heads = q_ref.shape[0]

  def block(key_start, n_keys, row_lo=0, row_hi=tq, masked=False, first=False):
    n_rows = row_hi - row_lo
    kv = kv_ref[pl.ds(key_start, n_keys), :]
    v_ones = jnp.concatenate([kv[:, :LANES], jnp.ones((n_keys, LANES), BF16)],
                             axis=1)
    s, p, alpha = {}, {}, {}
    if masked:
      visible = (lax.broadcasted_iota(jnp.int32, (n_rows, n_keys), 1)
                 <= lax.broadcasted_iota(jnp.int32, (n_rows, n_keys), 0))

    def rows_of(hd):
      return slice(hd * tq + row_lo, hd * tq + row_hi)

    def scores(hd):
      s[hd] = lax.dot_general(q_ref[hd, row_lo:row_hi, :], kv,
                              (((1,), (1,)), ((), ())),
                              preferred_element_type=F32)

    def softmax(hd):
      rows = rows_of(hd)
      sg = s.pop(hd)
      if masked:
        sg = jnp.where(visible, sg, MASK_VALUE)
      m_cur = jnp.max(sg, axis=1, keepdims=True)
      if first:
        m_new = jnp.broadcast_to(m_cur, (n_rows, LANES))
      else:
        m_prev = m_scr[rows]
        m_new = jnp.maximum(m_prev, m_cur)
        alpha[hd] = jnp.exp2(m_prev - m_new)
      ps = [jnp.exp2(sg[:, k * LANES:(k + 1) * LANES] - m_new)
            for k in range(n_keys // LANES)]
      p[hd] = jnp.concatenate(ps, axis=1).astype(BF16)
      m_scr[rows] = m_new

    def weighted_values(hd):
      rows = rows_of(hd)
      pv = jnp.dot(p.pop(hd), v_ones, preferred_element_type=F32)
      if first:
        acc_scr[rows] = pv[:, :LANES]
        l_scr[rows] = pv[:, LANES:]
      else:
        a = alpha.pop(hd)
        acc_scr[rows] = a * acc_scr[rows] + pv[:, :LANES]
        l_scr[rows] = a * l_scr[rows] + pv[:, LANES:]

    scores(0)
    for hd in range(heads):
      if hd + 1 < heads:
        scores(hd + 1)
      softmax(hd)
      weighted_values(hd)

  diag = pl.multiple_of(i * tq, tq)
  half = tq // 2
  block(diag, half, masked=True, first=True)
  block(diag + half, half, row_lo=half, masked=True)
  wide = ATTN_WIDE_BLOCKS * tq

  def wide_block(j, carry):
    block(pl.multiple_of(j * wide, wide), wide)
    return carry

  n_wide = i // ATTN_WIDE_BLOCKS
  lax.fori_loop(0, n_wide, wide_block, 0)
  for r in range(ATTN_WIDE_BLOCKS - 1):
    @pl.when(i - n_wide * ATTN_WIDE_BLOCKS > r)
    def _(r=r):
      block(pl.multiple_of((n_wide * ATTN_WIDE_BLOCKS + r) * tq, tq), tq)
  for hd in range(heads):
    rows = slice(hd * tq, (hd + 1) * tq)
    o_ref[:, hd * LANES:(hd + 1) * LANES] = (
        acc_scr[rows] / l_scr[rows]).astype(BF16)


def _mla_attention(q, kv):
  b, h, s, w = q.shape
  tq = ATTN_TILE
  rows = h * tq
  return pl.pallas_call(
      functools.partial(_flash_kernel, tq=tq),
      grid=(b, s // tq),
      in_specs=[
          pl.BlockSpec((None, h, tq, w), lambda bi, i: (bi, 0, i, 0)),
          pl.BlockSpec((None, s, w), lambda bi, i: (bi, 0, 0)),
      ],
      out_specs=pl.BlockSpec((None, tq, h * LANES), lambda bi, i: (bi, i, 0)),
      out_shape=jax.ShapeDtypeStruct((b, s, h * LANES), BF16),
      scratch_shapes=[pltpu.VMEM((rows, LANES), F32)] * 3,
      compiler_params=_params("arbitrary", "arbitrary"),
      name="mla_attention",
  )(q, kv)


def _mlp_core(x, mod_ref, g_ref, w1_ref, w2_ref, fg_ref, h_scr, acc_scr, tf,
              side_work=()):
  h_scr[...] = _norm_mod(x, g_ref[...], mod_ref[3:4, :],
                         mod_ref[4:5, :]).astype(BF16)
  d_ff = w1_ref.shape[1]
  n_chunks = d_ff // tf
  side_work = list(side_work)
  for j in range(n_chunks):
    cols = slice(j * tf, (j + 1) * tf)
    a = jnp.dot(h_scr[...], w1_ref[:, cols], preferred_element_type=F32)
    a = jnp.maximum(a, 0.0)
    part = jnp.dot((a * a).astype(BF16), w2_ref[cols, :],
                   preferred_element_type=F32)
    if j == 0:
      acc_scr[...] = part
    else:
      acc_scr[...] += part
    slots = max(n_chunks - 1, 1)
    lo = min(-(-j * len(side_work) // slots), len(side_work))
    hi = min(-(-(j + 1) * len(side_work) // slots), len(side_work))
    for work in side_work[lo:hi]:
      work()
  out = x + mod_ref[5:6, :] * acc_scr[...]
  if fg_ref is not None:
    out = _rms(out, RMS_EPS) * fg_ref[...]
  return out


def _mlp_stages(rows, get_x, mod_ref, g_ref, w1_ref, w2_ref, fg_ref, o_ref,
                h_scr, acc_scr, tf):
  def normalise():
    x = get_x()
    o_ref[rows, :] = x
    h_scr[rows, :] = _norm_mod(x, g_ref[...], mod_ref[3:4, :],
                               mod_ref[4:5, :]).astype(BF16)

  def chunk(j):
    cols = slice(j * tf, (j + 1) * tf)
    a = jnp.dot(h_scr[rows, :], w1_ref[:, cols], preferred_element_type=F32)
    a = jnp.maximum(a, 0.0)
    part = jnp.dot((a * a).astype(BF16), w2_ref[cols, :],
                   preferred_element_type=F32)
    if j == 0:
      acc_scr[rows, :] = part
    else:
      acc_scr[rows, :] += part

  def finish():
    out = o_ref[rows, :] + mod_ref[5:6, :] * acc_scr[rows, :]
    if fg_ref is not None:
      out = _rms(out, RMS_EPS) * fg_ref[...]
    o_ref[rows, :] = out

  n_chunks = w1_ref.shape[1] // tf
  return ([normalise] + [functools.partial(chunk, j) for j in range(n_chunks)]
          + [finish])


def _mlp_kernel(*refs, tm, tf, with_attn, final):
  refs = list(refs)
  x_ref, mod_ref, g_ref, w1_ref, w2_ref = refs[:5]
  rest = refs[5:]
  if with_attn:
    attn_ref, wo_ref = rest[:2]
    rest = rest[2:]
  fg_ref = None
  if final:
    fg_ref = rest[0]
    rest = rest[1:]
  o_ref, h_scr, acc_scr = rest

  def stages(rows):
    def get_x():
      x = x_ref[rows, :]
      if with_attn:
        y = jnp.dot(attn_ref[rows, :], wo_ref[...], preferred_element_type=F32)
        x = x + mod_ref[2:3, :] * y
      return x
    return _mlp_stages(rows, get_x, mod_ref, g_ref, w1_ref, w2_ref, fg_ref,
                       o_ref, h_scr, acc_scr, tf)

  half = tm // 2
  _staggered(stages(slice(0, half)), stages(slice(half, tm)))


def _layer_spec(shape, layer):
  zeros = (0,) * len(shape)
  return pl.BlockSpec((None,) + tuple(shape), lambda *_: (layer,) + zeros,
                      pipeline_mode=pl.Buffered(1))


def _mlp_layer(x, mod, layer, g, w1, w2, attn=None, final_g=None):
  b, s, d = x.shape
  tm = MLP_TOKEN_TILE
  d_ff = w1.shape[-1]
  with_attn = attn is not None
  final = final_g is not None
  in_specs = [
      pl.BlockSpec((None, tm, d), lambda bi, i: (bi, i, 0)),
      pl.BlockSpec((None, None, 6, d), lambda bi, i: (layer, bi, 0, 0)),
      _const_spec((1, d)),
      _layer_spec((d, d_ff), layer),
      _layer_spec((d_ff, d), layer),
  ]
  args = [x, mod, g.reshape(1, d), w1, w2]
  if with_attn:
    attn_out, w_ovo = attn
    in_specs += [
        pl.BlockSpec((None, tm, attn_out.shape[-1]), lambda bi, i: (bi, i, 0)),
        _const_spec(w_ovo.shape),
    ]
    args += [attn_out, w_ovo]
  if final:
    in_specs.append(_const_spec((1, d)))
    args.append(final_g.reshape(1, d))
  return pl.pallas_call(
      functools.partial(_mlp_kernel, tm=tm, tf=MLP_FF_TILE,
                        with_attn=with_attn, final=final),
      grid=(b, s // tm),
      in_specs=in_specs,
      out_specs=pl.BlockSpec((None, tm, d), lambda bi, i: (bi, i, 0)),
      out_shape=jax.ShapeDtypeStruct(x.shape, F32),
      scratch_shapes=[pltpu.VMEM((tm, d), BF16), pltpu.VMEM((tm, d), F32)],
      compiler_params=_params("arbitrary", "arbitrary"),
      name=f"mlp_{layer}",
  )(*args)


def _pool_mlp_kernel(*refs, tm, tf, tiles_per_seq, final):
  refs = list(refs)
  (x0_ref, xn_ref, xhn_ref, mod_ref, modn_ref, gmix_ref, pw_ref, ps_ref,
   g_ref, w1_ref, w2_ref) = refs[:11]
  rest = refs[11:]
  fg_ref = None
  if final:
    fg_ref = rest[0]
    rest = rest[1:]
  o_ref, x1_scr, a_scr, h_scr, acc_scr = rest
  t = pl.program_id(0)
  slot = t % 2

  @pl.when(t == 0)
  def _():
    for stage in _pool_stages(x0_ref, lambda: x0_ref[0:POOL_HALO, :], 0,
                              gmix_ref, mod_ref, pw_ref, ps_ref, a_scr,
                              x1_scr.at[0], tm):
      stage()

  next_stages = _pool_stages(xn_ref, lambda: xhn_ref[...],
                             (t + 1) % tiles_per_seq, gmix_ref, modn_ref,
                             pw_ref, ps_ref, a_scr, x1_scr.at[1 - slot], tm)
  o_ref[...] = _mlp_core(x1_scr[slot], mod_ref, g_ref, w1_ref, w2_ref, fg_ref,
                         h_scr, acc_scr, tf, side_work=next_stages)


def _pool_mlp_layer(x, mod, layer, g_mix, pool_w, pool_scale, g, w1, w2,
                    final_g=None):
  b, s, d = x.shape
  tm = TOKEN_TILE
  d_ff = w1.shape[-1]
  assert POOL_HALO >= max(POOL_WINDOWS) and tm % POOL_HALO == 0 and s % tm == 0
  tiles_per_seq = s // tm
  n_tiles = b * tiles_per_seq
  halo_per_tile = tm // POOL_HALO
  final = final_g is not None
  xf = x.reshape(b * s, d)

  def nxt(t):
    return jnp.minimum(t + 1, n_tiles - 1)

  in_specs = [
      pl.BlockSpec((tm, d), lambda t: (0, 0), pipeline_mode=pl.Buffered(1)),
      pl.BlockSpec((tm, d), lambda t: (nxt(t), 0)),
      pl.BlockSpec((POOL_HALO, d),
                   lambda t: (jnp.maximum(nxt(t) * halo_per_tile - 1, 0), 0)),
      pl.BlockSpec((None, None, 6, d),
                   lambda t: (layer, t // tiles_per_seq, 0, 0)),
      pl.BlockSpec((None, None, 6, d),
                   lambda t: (layer, nxt(t) // tiles_per_seq, 0, 0)),
      _const_spec((1, d)),
      _const_spec(pool_w.shape),
      _const_spec((1, d)),
      _const_spec((1, d)),
      _layer_spec((d, d_ff), layer),
      _layer_spec((d_ff, d), layer),
  ]
  args = [xf, xf, xf, mod, mod, g_mix.reshape(1, d), pool_w.astype(BF16),
          pool_scale.reshape(1, d), g.reshape(1, d), w1, w2]
  if final:
    in_specs.append(_const_spec((1, d)))
    args.append(final_g.reshape(1, d))
  out = pl.pallas_call(
      functools.partial(_pool_mlp_kernel, tm=tm, tf=MLP_FF_TILE,
                        tiles_per_seq=tiles_per_seq, final=final),
      grid=(n_tiles,),
      in_specs=in_specs,
      out_specs=pl.BlockSpec((tm, d), lambda t: (t, 0)),
      out_shape=jax.ShapeDtypeStruct(xf.shape, F32),
      scratch_shapes=[pltpu.VMEM((2, tm, d), F32),
                      pltpu.VMEM((POOL_HALO + tm, d), F32),
                      pltpu.VMEM((tm, d), BF16), pltpu.VMEM((tm, d), F32)],
      compiler_params=_params("arbitrary"),
      name=f"pool_mlp_{layer}",
  )(*args)
  return out.reshape(b, s, d)


def kernel(x, c, positions, ada_w, ada_b, norm_mix_g, norm_mlp_g, pool_w, pool_scale, sgu_w_in, sgu_ln_g, sgu_ln_b, sgu_w_s, sgu_b_s, sgu_w_out, mla_w_dq_dkv, mla_q_norm_g, mla_kv_norm_g, mla_w_uq, mla_w_ukv, mla_w_o, mlp_w1, mlp_w2, final_g):
  depth = ada_w.shape[0]
  mod = _ada_mod(c, ada_w, ada_b)
  w1, w2 = mlp_w1.astype(BF16), mlp_w2.astype(BF16)
  for i in range(depth):
    kind, j = i % N_MIXERS, i // N_MIXERS
    fg = final_g if i == depth - 1 else None
    if kind == 0:
      x = _pool_mlp_layer(x, mod, i, norm_mix_g[i], pool_w[j], pool_scale[j],
                          norm_mlp_g[i], w1, w2, final_g=fg)
      continue
    attn = None
    if kind == 1:
      x = _sgu_layer(x, mod, i, norm_mix_g[i], sgu_w_in[j], sgu_ln_g[j],
                     sgu_ln_b[j], sgu_w_s[j], sgu_b_s[j], sgu_w_out[j])
    else:
      w_q, w_ovo = _mla_fold(mla_w_uq[j], mla_w_ukv[j], mla_w_o[j])
      q, kv = _mla_proj(x, mod, i, norm_mix_g[i], positions, mla_w_dq_dkv[j],
                        mla_q_norm_g[j], mla_kv_norm_g[j], w_q)
      attn = (_mla_attention(q, kv), w_ovo)
    x = _mlp_layer(x, mod, i, norm_mlp_g[i], w1, w2, attn=attn, final_g=fg)
  return x
```

```python
import functools

import jax
import jax.numpy as jnp
from jax import lax
from jax.experimental import pallas as pl
from jax.experimental.pallas import tpu as pltpu

F32 = jnp.float32
BF16 = jnp.bfloat16

POOL_WINDOWS = (2, 4, 8, 16)
SGU_CHUNK = 128
SGU_HEAD_DIM = 128
MLA_HEADS = 16
MLA_Q_LORA = 256
MLA_KV_LORA = 128
MLA_NOPE = 128
MLA_ROPE = 64
MLA_V = 128
ROPE_THETA = 10000.0
RMS_EPS = 1e-6
LN_EPS = 1e-5
N_MIXERS = 3
MASK_VALUE = -1e30
LOG2_E = 1.4426950408889634
Q_SCALE = (MLA_NOPE + MLA_ROPE) ** -0.5 * LOG2_E

LANES = 128
VMEM_LIMIT_BYTES = 56 * 1024 * 1024

TOKEN_TILE = 512
MLP_TOKEN_TILE = 1024
SGU_TOKEN_TILE = 1024
PROJ_TOKEN_TILE = 1024
POOL_HALO = 16
ADA_COL_TILE = 2048
MLP_FF_TILE = 1024
ATTN_TILE = 512
ATTN_WIDE_BLOCKS = 2
QK_WIDTH = 2 * LANES


def _params(*semantics):
  return pltpu.CompilerParams(dimension_semantics=semantics,
                              vmem_limit_bytes=VMEM_LIMIT_BYTES)


def _const_spec(shape):
  zeros = (0,) * len(shape)
  return pl.BlockSpec(shape, lambda *_: zeros, pipeline_mode=pl.Buffered(1))


def _rms(x, eps):
  return x * lax.rsqrt(jnp.mean(x * x, axis=-1, keepdims=True) + eps)


def _norm_mod(x, g, shift, scale):
  return _rms(x, RMS_EPS) * g * (1.0 + scale) + shift


def _gelu(x):
  return 0.5 * x * (1.0 + lax.erf(x * (2.0 ** -0.5)))


def _staggered(stages_a, stages_b):
  n = len(stages_a)
  for k in range(n + 1):
    if k < n:
      stages_a[k]()
    if k >= 1:
      stages_b[k - 1]()


class _WeightCast:
  def __init__(self, w1_all, w2_all, layer, n_steps, step_of):
    self.args = [w1_all, w2_all]
    self.in_specs, self.out_specs, self.out_shape = [], [], []
    for w in self.args:
      _, rows, cols = w.shape
      assert rows % n_steps == 0
      blk = rows // n_steps
      self.in_specs.append(pl.BlockSpec(
          (None, blk, cols), lambda *g: (layer, step_of(*g), 0)))
      self.out_specs.append(pl.BlockSpec(
          (blk, cols), lambda *g: (step_of(*g), 0)))
      self.out_shape.append(jax.ShapeDtypeStruct((rows, cols), BF16))

  @staticmethod
  def run(in_refs, out_refs):
    for src, dst in zip(in_refs, out_refs):
      dst[...] = src[...].astype(BF16)


def _ada_kernel(c_ref, w_ref, b_ref, o_ref):
  c = c_ref[...]
  ca = (c * jax.nn.sigmoid(c)).astype(BF16)
  o_ref[...] = jnp.dot(ca, w_ref[...].astype(BF16),
                       preferred_element_type=F32) + b_ref[...]


def _ada_mod(c, ada_w, ada_b):
  depth, d, d6 = ada_w.shape
  b = c.shape[0]
  tn = ADA_COL_TILE
  out = pl.pallas_call(
      _ada_kernel,
      grid=(depth, d6 // tn),
      in_specs=[
          pl.BlockSpec((b, d), lambda l, j: (0, 0)),
          pl.BlockSpec((None, d, tn), lambda l, j: (l, 0, j)),
          pl.BlockSpec((None, 1, tn), lambda l, j: (l, 0, j)),
      ],
      out_specs=pl.BlockSpec((None, b, tn), lambda l, j: (l, 0, j)),
      out_shape=jax.ShapeDtypeStruct((depth, b, d6), F32),
      compiler_params=_params("arbitrary", "arbitrary"),
      name="ada_mod",
  )(c, ada_w, ada_b.reshape(depth, 1, d6))
  return out.reshape(depth, b, 6, d)


def _pool_stages(x_ref, xh, i, g_ref, mod_ref, w_ref, ps_ref, a_scr, out_ref,
                 tm):
  def prep():
    g, shift, scale = g_ref[...], mod_ref[0:1, :], mod_ref[1:2, :]
    hh = _norm_mod(xh(), g, shift, scale)
    a_scr[0:POOL_HALO, :] = jnp.where(i > 0, hh, 0.0)
    a_scr[POOL_HALO:, :] = _norm_mod(x_ref[...], g, shift, scale)

  def group(gi, win):
    gdim = w_ref.shape[1]
    cols = slice(gi * gdim, (gi + 1) * gdim)
    s = a_scr[:, cols]
    k = 1
    while k < win:
      s = s + pltpu.roll(s, k, axis=0)
      k *= 2
    t = (i * tm + lax.broadcasted_iota(jnp.int32, (tm, 1), 0)).astype(F32)
    inv_count = 1.0 / jnp.minimum(t + 1.0, float(win))
    pooled = s[POOL_HALO:] * inv_count - a_scr[POOL_HALO:, cols]
    y = jnp.dot(pooled.astype(BF16), w_ref[gi], preferred_element_type=F32)
    out_ref[:, cols] = x_ref[:, cols] + mod_ref[2:3, cols] * (
        y * ps_ref[:, cols])

  return [prep] + [functools.partial(group, gi, win)
                   for gi, win in enumerate(POOL_WINDOWS)]


def _sgu_stages(rows, n_rows, x_ref, mod_ref, g_ref, win_ref, lng_ref, lnb_ref,
                ws_ref, bst_ref, wout_ref, o_ref, gated_scr):
  st = {}
  width = wout_ref.shape[0]
  t = SGU_CHUNK

  def project_v():
    st["h"] = _norm_mod(x_ref[rows, :], g_ref[...], mod_ref[0:1, :],
                        mod_ref[1:2, :]).astype(BF16)
    st["v"] = jnp.dot(st["h"], win_ref[:, width:], preferred_element_type=F32)

  def project_u():
    st["u"] = _gelu(jnp.dot(st["h"], win_ref[:, :width],
                            preferred_element_type=F32))

  def normalise():
    v = _gelu(st["v"])
    mu = jnp.mean(v, axis=-1, keepdims=True)
    vc = v - mu
    var = jnp.mean(vc * vc, axis=-1, keepdims=True)
    st["v"] = (vc * lax.rsqrt(var + LN_EPS) * lng_ref[...]
               + lnb_ref[...]).astype(BF16)

  def gate_spatial():
    u, v = st["u"], st["v"]
    causal = (lax.broadcasted_iota(jnp.int32, (t, t), 1)
              <= lax.broadcasted_iota(jnp.int32, (t, t), 0))
    for hd in range(ws_ref.shape[0]):
      cols = slice(hd * SGU_HEAD_DIM, (hd + 1) * SGU_HEAD_DIM)
      ws = jnp.where(causal, ws_ref[hd], 0.0).astype(BF16)
      bias = bst_ref[:, hd:hd + 1]
      for c in range(n_rows // t):
        blk = slice(c * t, (c + 1) * t)
        mixed = jnp.dot(ws, v[blk, cols], preferred_element_type=F32) + bias
        gated_scr[rows.start + c * t:rows.start + (c + 1) * t, cols] = (
            u[blk, cols] * mixed).astype(BF16)

  def project_out():
    y = jnp.dot(gated_scr[rows, :], wout_ref[...], preferred_element_type=F32)
    o_ref[rows, :] = x_ref[rows, :] + mod_ref[2:3, :] * y

  return [project_v, project_u, normalise, gate_spatial, project_out]


def _sgu_kernel(*refs, tm, n_cast):
  n_in = 9
  ins, cast_in = refs[:n_in], refs[n_in:n_in + n_cast]
  o_ref = refs[n_in + n_cast]
  cast_out = refs[n_in + n_cast + 1:n_in + 2 * n_cast + 1]
  gated_scr = refs[-1]
  _WeightCast.run(cast_in, cast_out)
  half = tm // 2
  _staggered(_sgu_stages(slice(0, half), half, *ins, o_ref, gated_scr),
             _sgu_stages(slice(half, tm), half, *ins, o_ref, gated_scr))


def _sgu_layer(x, mod, layer, g, w_in, ln_g, ln_b, w_s, b_s, w_out,
               next_mlp_weights=None):
  b, s, d = x.shape
  tm = SGU_TOKEN_TILE
  width = w_out.shape[0]
  heads, t, _ = w_s.shape
  assert t == SGU_CHUNK and tm % t == 0 and heads * SGU_HEAD_DIM == width
  n_i = s // tm
  in_specs = [
      pl.BlockSpec((None, tm, d), lambda bi, i: (bi, i, 0)),
      pl.BlockSpec((None, None, 6, d), lambda bi, i: (layer, bi, 0, 0)),
      _const_spec((1, d)),
      _const_spec((d, 2 * width)),
      _const_spec((1, width)),
      _const_spec((1, width)),
      _const_spec((heads, t, t)),
      _const_spec((t, heads)),
      _const_spec((width, d)),
  ]
  args = [x, mod, g.reshape(1, d), w_in.astype(BF16), ln_g.reshape(1, width),
          ln_b.reshape(1, width), w_s, b_s.T, w_out.astype(BF16)]
  out_specs = [pl.BlockSpec((None, tm, d), lambda bi, i: (bi, i, 0))]
  out_shape = [jax.ShapeDtypeStruct(x.shape, F32)]
  n_cast = 0
  if next_mlp_weights is not None:
    w1_all, w2_all, nxt = next_mlp_weights
    cast = _WeightCast(w1_all, w2_all, nxt, b * n_i,
                       lambda bi, i: bi * n_i + i)
    in_specs += cast.in_specs
    args += cast.args
    out_specs += cast.out_specs
    out_shape += cast.out_shape
    n_cast = len(cast.args)
  outs = pl.pallas_call(
      functools.partial(_sgu_kernel, tm=tm, n_cast=n_cast),
      grid=(b, n_i),
      in_specs=in_specs,
      out_specs=out_specs,
      out_shape=out_shape,
      scratch_shapes=[pltpu.VMEM((tm, width), BF16)],
      compiler_params=_params("arbitrary", "arbitrary"),
      name=f"sgu_mixer_{layer}",
  )(*args)
  return outs[0], tuple(outs[1:])


def _mla_fold_kernel(wqn_ref, wuk_ref, wuv_ref, wo_ref, qabs_ref, ovo_ref):
  qabs = lax.dot_general(
      wqn_ref[...], wuk_ref[...], (((1,), (1,)), ((), ())),
      precision=lax.Precision.HIGHEST, preferred_element_type=F32)
  qabs_ref[...] = (qabs * Q_SCALE).astype(BF16)
  ovo_ref[...] = jnp.dot(wuv_ref[...], wo_ref[...],
                         precision=lax.Precision.HIGHEST,
                         preferred_element_type=F32).astype(BF16)


def _mla_fold(w_uq, w_ukv, w_o):
  h, ql, kvl = MLA_HEADS, MLA_Q_LORA, MLA_KV_LORA
  d = w_o.shape[1]
  w_uq3 = w_uq.reshape(ql, h, MLA_NOPE + MLA_ROPE)
  w_ukv3 = w_ukv.reshape(kvl, h, MLA_NOPE + MLA_V)
  wqn = w_uq3[:, :, :MLA_NOPE].transpose(1, 0, 2)
  wuk = w_ukv3[:, :, :MLA_NOPE].transpose(1, 0, 2)
  wuv = w_ukv3[:, :, MLA_NOPE:].transpose(1, 0, 2)
  wo = w_o.reshape(h, MLA_V, d)
  qabs, ovo = pl.pallas_call(
      _mla_fold_kernel,
      grid=(h,),
      in_specs=[
          pl.BlockSpec((None, ql, MLA_NOPE), lambda i: (i, 0, 0)),
          pl.BlockSpec((None, kvl, MLA_NOPE), lambda i: (i, 0, 0)),
          pl.BlockSpec((None, kvl, MLA_V), lambda i: (i, 0, 0)),
          pl.BlockSpec((None, MLA_V, d), lambda i: (i, 0, 0)),
      ],
      out_specs=[
          pl.BlockSpec((None, ql, kvl), lambda i: (i, 0, 0)),
          pl.BlockSpec((kvl, d), lambda i: (i, 0)),
      ],
      out_shape=[
          jax.ShapeDtypeStruct((h, ql, kvl), BF16),
          jax.ShapeDtypeStruct((h * kvl, d), BF16),
      ],
      compiler_params=_params("arbitrary"),
      name="mla_fold",
  )(wqn, wuk, wuv, wo)
  half = MLA_ROPE // 2
  wr = w_uq3[:, :, MLA_NOPE:] * Q_SCALE
  wr_swapped = jnp.concatenate([wr[:, :, half:], wr[:, :, :half]], axis=-1)
  w_q = jnp.concatenate([
      qabs.transpose(1, 0, 2).reshape(ql, h * kvl),
      wr.reshape(ql, h * MLA_ROPE).astype(BF16),
      wr_swapped.reshape(ql, h * MLA_ROPE).astype(BF16),
  ], axis=1)
  return w_q, ovo


def _mla_proj_stages(rows, x_ref, mod_ref, g_ref, pos_ref, wlat_ref, qg_ref,
                     kvg_ref, wq_ref, freq_ref, sign_ref, q_ref, kv_ref):
  st = {}
  ql, kvl = MLA_Q_LORA, MLA_KV_LORA
  n = rows.stop - rows.start
  mid = rows.start + n // 2
  heads = q_ref.shape[0]
  hw = heads * LANES

  def project_latent():
    h = _norm_mod(x_ref[rows, :], g_ref[...], mod_ref[0:1, :],
                  mod_ref[1:2, :]).astype(BF16)
    st["lat"] = jnp.dot(h, wlat_ref[...], preferred_element_type=F32)

  def rotary_and_keys():
    lat = st["lat"]
    st["cq"] = (_rms(lat[:, :ql], RMS_EPS) * qg_ref[...]).astype(BF16)
    ckv = _rms(lat[:, ql:ql + kvl], RMS_EPS) * kvg_ref[...]
    low = lax.broadcasted_iota(jnp.int32, (1, LANES), 1) < LANES // 2
    pos = jnp.where(low, pos_ref[rows.start:mid, :], pos_ref[mid:rows.stop, :])
    ang = pos.astype(F32) * freq_ref[...]

    def spread(t):
      r = pltpu.roll(t, LANES // 2, axis=1)
      return jnp.concatenate([jnp.where(low, t, r), jnp.where(low, r, t)],
                             axis=0)

    cos = spread(jnp.cos(ang))
    sin = spread(jnp.sin(ang)) * sign_ref[...]
    kra = lat[:, ql + kvl:ql + kvl + LANES]
    krb = lat[:, ql + kvl + LANES:]
    kv_ref[rows, :LANES] = ckv.astype(BF16)
    kv_ref[rows, LANES:] = (kra * cos + krb * sin).astype(BF16)
    st["cos2"] = jnp.concatenate([cos, cos], axis=1)
    st["sin2"] = jnp.concatenate([sin, sin], axis=1)

  def queries(quads):
    cq = st["cq"]
    hr = heads * MLA_ROPE
    for u in quads:
      r0 = hw + 2 * u * LANES
      ra = jnp.dot(cq, wq_ref[:, r0:r0 + 2 * LANES],
                   preferred_element_type=F32)
      rb = jnp.dot(cq, wq_ref[:, hr + r0:hr + r0 + 2 * LANES],
                   preferred_element_type=F32)
      qr = ra * st["cos2"] + rb * st["sin2"]
      for v in range(2):
        c0 = (4 * u + 2 * v) * LANES
        qa = jnp.dot(cq, wq_ref[:, c0:c0 + 2 * LANES],
                     preferred_element_type=F32)
        pair = qr[:, v * LANES:(v + 1) * LANES]
        tiles = (pair, pltpu.roll(pair, LANES // 2, axis=1))
        for k in range(2):
          hd = 4 * u + 2 * v + k
          q_ref[hd, rows, :LANES] = qa[:, k * LANES:(k + 1) * LANES].astype(BF16)
          q_ref[hd, rows, LANES:] = tiles[k].astype(BF16)

  n_quads = heads // 4
  return [project_latent, rotary_and_keys,
          functools.partial(queries, range(0, n_quads // 2)),
          functools.partial(queries, range(n_quads // 2, n_quads))]


def _mla_proj_kernel(*refs, tm):
  half = tm // 2
  _staggered(_mla_proj_stages(slice(0, half), *refs),
             _mla_proj_stages(slice(half, tm), *refs))


def _mla_proj(x, mod, layer, g, positions, w_dq_dkv, q_norm_g, kv_norm_g, w_q):
  b, s, d = x.shape
  tm = PROJ_TOKEN_TILE
  ql, kvl, rope, h = MLA_Q_LORA, MLA_KV_LORA, MLA_ROPE, MLA_HEADS
  half = rope // 2
  assert 2 * rope == LANES and h % 4 == 0
  wkr = w_dq_dkv[:, ql + kvl:]
  zeros = jnp.zeros((d, LANES - rope), F32)
  w_lat = jnp.concatenate(
      [w_dq_dkv[:, :ql + kvl], wkr, zeros, wkr[:, half:], wkr[:, :half], zeros],
      axis=1).astype(BF16)
  inv_freq = ROPE_THETA ** (-jnp.arange(0, rope, 2, dtype=F32) / rope)
  freq = jnp.tile(inv_freq, LANES // half).reshape(1, LANES)
  sign = jnp.tile(jnp.concatenate([-jnp.ones((half,), F32),
                                   jnp.ones((half,), F32)]),
                  LANES // rope).reshape(1, LANES)
  return pl.pallas_call(
      functools.partial(_mla_proj_kernel, tm=tm),
      grid=(b, s // tm),
      in_specs=[
          pl.BlockSpec((None, tm, d), lambda bi, i: (bi, i, 0)),
          pl.BlockSpec((None, None, 6, d), lambda bi, i: (layer, bi, 0, 0)),
          _const_spec((1, d)),
          pl.BlockSpec((None, tm, 1), lambda bi, i: (bi, i, 0)),
          _const_spec(w_lat.shape),
          _const_spec((1, ql)),
          _const_spec((1, kvl)),
          _const_spec(w_q.shape),
          _const_spec((1, LANES)),
          _const_spec((1, LANES)),
      ],
      out_specs=[
          pl.BlockSpec((None, h, tm, QK_WIDTH), lambda bi, i: (bi, 0, i, 0)),
          pl.BlockSpec((None, tm, QK_WIDTH), lambda bi, i: (bi, i, 0)),
      ],
      out_shape=[
          jax.ShapeDtypeStruct((b, h, s, QK_WIDTH), BF16),
          jax.ShapeDtypeStruct((b, s, QK_WIDTH), BF16),
      ],
      compiler_params=_params("arbitrary", "arbitrary"),
      name=f"mla_proj_{layer}",
  )(x, mod, g.reshape(1, d), positions.reshape(b, s, 1), w_lat,
    q_norm_g.reshape(1, ql), kv_norm_g.reshape(1, kvl), w_q, freq, sign)


def _flash_kernel(*refs, tq, n_cast):
  q_ref, kv_ref = refs[:2]
  cast_in = refs[2:2 + n_cast]
  o_ref = refs[2 + n_cast]
  cast_out = refs[3 + n_cast:3 + 2 * n_cast]
  m_scr, l_scr, acc_scr = refs[-3:]
  _WeightCast.run(cast_in, cast_out)
  i = pl.program_id(1)
  heads = q_ref.shape[0]

  def block(key_start, n_keys, row_lo=0, row_hi=tq, masked=False, first=False):
    n_rows = row_hi - row_lo
    kv = kv_ref[pl.ds(key_start, n_keys), :]
    v_ones = jnp.concatenate([kv[:, :LANES], jnp.ones((n_keys, LANES), BF16)],
                             axis=1)
    s, p, alpha = {}, {}, {}
    if masked:
      visible = (lax.broadcasted_iota(jnp.int32, (n_rows, n_keys), 1)
                 <= lax.broadcasted_iota(jnp.int32, (n_rows, n_keys), 0))

    def rows_of(hd):
      return slice(hd * tq + row_lo, hd * tq + row_hi)

    def scores(hd):
      s[hd] = lax.dot_general(q_ref[hd, row_lo:row_hi, :], kv,
                              (((1,), (1,)), ((), ())),
                              preferred_element_type=F32)

    def softmax(hd):
      rows = rows_of(hd)
      sg = s.pop(hd)
      if masked:
        sg = jnp.where(visible, sg, MASK_VALUE)
      m_cur = jnp.max(sg, axis=1, keepdims=True)
      if first:
        m_new = jnp.broadcast_to(m_cur, (n_rows, LANES))
      else:
        m_prev = m_scr[rows]
        m_new = jnp.maximum(m_prev, m_cur)
        alpha[hd] = jnp.exp2(m_prev - m_new)
      ps = [jnp.exp2(sg[:, k * LANES:(k + 1) * LANES] - m_new)
            for k in range(n_keys // LANES)]
      p[hd] = jnp.concatenate(ps, axis=1).astype(BF16)
      m_scr[rows] = m_new

    def weighted_values(hd):
      rows = rows_of(hd)
      pv = jnp.dot(p.pop(hd), v_ones, preferred_element_type=F32)
      if first:
        acc_scr[rows] = pv[:, :LANES]
        l_scr[rows] = pv[:, LANES:]
      else:
        a = alpha.pop(hd)
        acc_scr[rows] = a * acc_scr[rows] + pv[:, :LANES]
        l_scr[rows] = a * l_scr[rows] + pv[:, LANES:]

    scores(0)
    for hd in range(heads):
      if hd + 1 < heads:
        scores(hd + 1)
      softmax(hd)
      weighted_values(hd)

  diag = pl.multiple_of(i * tq, tq)
  half = tq // 2
  block(diag, half, masked=True, first=True)
  block(diag + half, half, row_lo=half, masked=True)
  wide = ATTN_WIDE_BLOCKS * tq

  def wide_block(j, carry):
    block(pl.multiple_of(j * wide, wide), wide)
    return carry

  n_wide = i // ATTN_WIDE_BLOCKS
  lax.fori_loop(0, n_wide, wide_block, 0)
  for r in range(ATTN_WIDE_BLOCKS - 1):
    @pl.when(i - n_wide * ATTN_WIDE_BLOCKS > r)
    def _(r=r):
      block(pl.multiple_of((n_wide * ATTN_WIDE_BLOCKS + r) * tq, tq), tq)
  for hd in range(heads):
    rows = slice(hd * tq, (hd + 1) * tq)
    o_ref[:, hd * LANES:(hd + 1) * LANES] = (
        acc_scr[rows] / l_scr[rows]).astype(BF16)


def _mla_attention(q, kv, next_mlp_weights=None):
  b, h, s, w = q.shape
  tq = ATTN_TILE
  rows = h * tq
  n_i = s // tq
  in_specs = [
      pl.BlockSpec((None, h, tq, w), lambda bi, i: (bi, 0, i, 0)),
      pl.BlockSpec((None, s, w), lambda bi, i: (bi, 0, 0)),
  ]
  args = [q, kv]
  out_specs = [pl.BlockSpec((None, tq, h * LANES), lambda bi, i: (bi, i, 0))]
  out_shape = [jax.ShapeDtypeStruct((b, s, h * LANES), BF16)]
  n_cast = 0
  if next_mlp_weights is not None:
    w1_all, w2_all, nxt = next_mlp_weights
    cast = _WeightCast(w1_all, w2_all, nxt, b * n_i,
                       lambda bi, i: bi * n_i + i)
    in_specs += cast.in_specs
    args += cast.args
    out_specs += cast.out_specs
    out_shape += cast.out_shape
    n_cast = len(cast.args)
  outs = pl.pallas_call(
      functools.partial(_flash_kernel, tq=tq, n_cast=n_cast),
      grid=(b, n_i),
      in_specs=in_specs,
      out_specs=out_specs,
      out_shape=out_shape,
      scratch_shapes=[pltpu.VMEM((rows, LANES), F32)] * 3,
      compiler_params=_params("arbitrary", "arbitrary"),
      name="mla_attention",
  )(*args)
  return outs[0], tuple(outs[1:])


def _mlp_core(x, mod_ref, g_ref, w1_ref, w2_ref, fg_ref, h_scr, acc_scr, tf,
              side_work=()):
  h_scr[...] = _norm_mod(x, g_ref[...], mod_ref[3:4, :],
                         mod_ref[4:5, :]).astype(BF16)
  d_ff = w1_ref.shape[1]
  n_chunks = d_ff // tf
  side_work = list(side_work)
  for j in range(n_chunks):
    cols = slice(j * tf, (j + 1) * tf)
    a = jnp.dot(h_scr[...], w1_ref[:, cols], preferred_element_type=F32)
    a = jnp.maximum(a, 0.0)
    part = jnp.dot((a * a).astype(BF16), w2_ref[cols, :],
                   preferred_element_type=F32)
    if j == 0:
      acc_scr[...] = part
    else:
      acc_scr[...] += part
    slots = max(n_chunks - 1, 1)
    lo = min(-(-j * len(side_work) // slots), len(side_work))
    hi = min(-(-(j + 1) * len(side_work) // slots), len(side_work))
    for work in side_work[lo:hi]:
      work()
  out = x + mod_ref[5:6, :] * acc_scr[...]
  if fg_ref is not None:
    out = _rms(out, RMS_EPS) * fg_ref[...]
  return out


def _mlp_stages(rows, get_x, mod_ref, g_ref, w1_ref, w2_ref, fg_ref, o_ref,
                h_scr, acc_scr, tf):
  def normalise():
    x = get_x()
    o_ref[rows, :] = x
    h_scr[rows, :] = _norm_mod(x, g_ref[...], mod_ref[3:4, :],
                               mod_ref[4:5, :]).astype(BF16)

  def chunk(j):
    cols = slice(j * tf, (j + 1) * tf)
    a = jnp.dot(h_scr[rows, :], w1_ref[:, cols], preferred_element_type=F32)
    a = jnp.maximum(a, 0.0)
    part = jnp.dot((a * a).astype(BF16), w2_ref[cols, :],
                   preferred_element_type=F32)
    if j == 0:
      acc_scr[rows, :] = part
    else:
      acc_scr[rows, :] += part

  def finish():
    out = o_ref[rows, :] + mod_ref[5:6, :] * acc_scr[rows, :]
    if fg_ref is not None:
      out = _rms(out, RMS_EPS) * fg_ref[...]
    o_ref[rows, :] = out

  n_chunks = w1_ref.shape[1] // tf
  return ([normalise] + [functools.partial(chunk, j) for j in range(n_chunks)]
          + [finish])


def _mlp_kernel(*refs, tm, tf, with_attn, final):
  refs = list(refs)
  x_ref, mod_ref, g_ref, w1_ref, w2_ref = refs[:5]
  rest = refs[5:]
  if with_attn:
    attn_ref, wo_ref = rest[:2]
    rest = rest[2:]
  fg_ref = None
  if final:
    fg_ref = rest[0]
    rest = rest[1:]
  o_ref, h_scr, acc_scr = rest

  def stages(rows):
    def get_x():
      x = x_ref[rows, :]
      if with_attn:
        y = jnp.dot(attn_ref[rows, :], wo_ref[...], preferred_element_type=F32)
        x = x + mod_ref[2:3, :] * y
      return x
    return _mlp_stages(rows, get_x, mod_ref, g_ref, w1_ref, w2_ref, fg_ref,
                       o_ref, h_scr, acc_scr, tf)

  half = tm // 2
  _staggered(stages(slice(0, half)), stages(slice(half, tm)))


def _mlp_layer(x, mod, layer, g, w1, w2, attn=None, final_g=None):
  b, s, d = x.shape
  tm = MLP_TOKEN_TILE
  d_ff = w1.shape[-1]
  with_attn = attn is not None
  final = final_g is not None
  in_specs = [
      pl.BlockSpec((None, tm, d), lambda bi, i: (bi, i, 0)),
      pl.BlockSpec((None, None, 6, d), lambda bi, i: (layer, bi, 0, 0)),
      _const_spec((1, d)),
      _const_spec((d, d_ff)),
      _const_spec((d_ff, d)),
  ]
  args = [x, mod, g.reshape(1, d), w1, w2]
  if with_attn:
    attn_out, w_ovo = attn
    in_specs += [
        pl.BlockSpec((None, tm, attn_out.shape[-1]), lambda bi, i: (bi, i, 0)),
        _const_spec(w_ovo.shape),
    ]
    args += [attn_out, w_ovo]
  if final:
    in_specs.append(_const_spec((1, d)))
    args.append(final_g.reshape(1, d))
  return pl.pallas_call(
      functools.partial(_mlp_kernel, tm=tm, tf=MLP_FF_TILE,
                        with_attn=with_attn, final=final),
      grid=(b, s // tm),
      in_specs=in_specs,
      out_specs=pl.BlockSpec((None, tm, d), lambda bi, i: (bi, i, 0)),
      out_shape=jax.ShapeDtypeStruct(x.shape, F32),
      scratch_shapes=[pltpu.VMEM((tm, d), BF16), pltpu.VMEM((tm, d), F32)],
      compiler_params=_params("arbitrary", "arbitrary"),
      name=f"mlp_{layer}",
  )(*args)


def _pool_mlp_kernel(*refs, tm, tf, tiles_per_seq, final, n_cast):
  refs = list(refs)
  (x0_ref, xn_ref, xhn_ref, mod_ref, modn_ref, gmix_ref, pw_ref, ps_ref,
   g_ref, w1_ref, w2_ref) = refs[:11]
  rest = refs[11:]
  fg_ref = None
  if final:
    fg_ref = rest[0]
    rest = rest[1:]
  cast_in, rest = rest[:n_cast], rest[n_cast:]
  o_ref, rest = rest[0], rest[1:]
  cast_out, rest = rest[:n_cast], rest[n_cast:]
  x1_scr, a_scr, h_scr, acc_scr = rest
  _WeightCast.run(cast_in, cast_out)
  t = pl.program_id(0)
  slot = t % 2

  @pl.when(t == 0)
  def _():
    for stage in _pool_stages(x0_ref, lambda: x0_ref[0:POOL_HALO, :], 0,
                              gmix_ref, mod_ref, pw_ref, ps_ref, a_scr,
                              x1_scr.at[0], tm):
      stage()

  next_stages = _pool_stages(xn_ref, lambda: xhn_ref[...],
                             (t + 1) % tiles_per_seq, gmix_ref, modn_ref,
                             pw_ref, ps_ref, a_scr, x1_scr.at[1 - slot], tm)
  o_ref[...] = _mlp_core(x1_scr[slot], mod_ref, g_ref, w1_ref, w2_ref, fg_ref,
                         h_scr, acc_scr, tf, side_work=next_stages)


def _pool_mlp_layer(x, mod, layer, g_mix, pool_w, pool_scale, g, w1, w2,
                    final_g=None, next_mlp_weights=None):
  b, s, d = x.shape
  tm = TOKEN_TILE
  d_ff = w1.shape[-1]
  assert POOL_HALO >= max(POOL_WINDOWS) and tm % POOL_HALO == 0 and s % tm == 0
  tiles_per_seq = s // tm
  n_tiles = b * tiles_per_seq
  halo_per_tile = tm // POOL_HALO
  final = final_g is not None
  xf = x.reshape(b * s, d)

  def nxt(t):
    return jnp.minimum(t + 1, n_tiles - 1)

  in_specs = [
      pl.BlockSpec((tm, d), lambda t: (0, 0), pipeline_mode=pl.Buffered(1)),
      pl.BlockSpec((tm, d), lambda t: (nxt(t), 0)),
      pl.BlockSpec((POOL_HALO, d),
                   lambda t: (jnp.maximum(nxt(t) * halo_per_tile - 1, 0), 0)),
      pl.BlockSpec((None, None, 6, d),
                   lambda t: (layer, t // tiles_per_seq, 0, 0)),
      pl.BlockSpec((None, None, 6, d),
                   lambda t: (layer, nxt(t) // tiles_per_seq, 0, 0)),
      _const_spec((1, d)),
      _const_spec(pool_w.shape),
      _const_spec((1, d)),
      _const_spec((1, d)),
      _const_spec((d, d_ff)),
      _const_spec((d_ff, d)),
  ]
  args = [xf, xf, xf, mod, mod, g_mix.reshape(1, d), pool_w.astype(BF16),
          pool_scale.reshape(1, d), g.reshape(1, d), w1, w2]
  if final:
    in_specs.append(_const_spec((1, d)))
    args.append(final_g.reshape(1, d))
  out_specs = [pl.BlockSpec((tm, d), lambda t: (t, 0))]
  out_shape = [jax.ShapeDtypeStruct(xf.shape, F32)]
  n_cast = 0
  if next_mlp_weights is not None:
    w1_all, w2_all, nxt_layer = next_mlp_weights
    cast = _WeightCast(w1_all, w2_all, nxt_layer, n_tiles, lambda t: t)
    in_specs += cast.in_specs
    args += cast.args
    out_specs += cast.out_specs
    out_shape += cast.out_shape
    n_cast = len(cast.args)
  outs = pl.pallas_call(
      functools.partial(_pool_mlp_kernel, tm=tm, tf=MLP_FF_TILE,
                        tiles_per_seq=tiles_per_seq, final=final,
                        n_cast=n_cast),
      grid=(n_tiles,),
      in_specs=in_specs,
      out_specs=out_specs,
      out_shape=out_shape,
      scratch_shapes=[pltpu.VMEM((2, tm, d), F32),
                      pltpu.VMEM((POOL_HALO + tm, d), F32),
                      pltpu.VMEM((tm, d), BF16), pltpu.VMEM((tm, d), F32)],
      compiler_params=_params("arbitrary"),
      name=f"pool_mlp_{layer}",
  )(*args)
  return outs[0].reshape(b, s, d), tuple(outs[1:])


def kernel(x, c, positions, ada_w, ada_b, norm_mix_g, norm_mlp_g, pool_w, pool_scale, sgu_w_in, sgu_ln_g, sgu_ln_b, sgu_w_s, sgu_b_s, sgu_w_out, mla_w_dq_dkv, mla_q_norm_g, mla_kv_norm_g, mla_w_uq, mla_w_ukv, mla_w_o, mlp_w1, mlp_w2, final_g):
  depth = ada_w.shape[0]
  mod = _ada_mod(c, ada_w, ada_b)
  w1, w2 = mlp_w1[0].astype(BF16), mlp_w2[0].astype(BF16)
  for i in range(depth):
    kind, j = i % N_MIXERS, i // N_MIXERS
    last = i == depth - 1
    fg = final_g if last else None
    nxt = None if last else (mlp_w1, mlp_w2, i + 1)
    if kind == 0:
      x, w_next = _pool_mlp_layer(
          x, mod, i, norm_mix_g[i], pool_w[j], pool_scale[j], norm_mlp_g[i],
          w1, w2, final_g=fg, next_mlp_weights=nxt)
    else:
      attn = None
      if kind == 1:
        x, w_next = _sgu_layer(
            x, mod, i, norm_mix_g[i], sgu_w_in[j], sgu_ln_g[j], sgu_ln_b[j],
            sgu_w_s[j], sgu_b_s[j], sgu_w_out[j], next_mlp_weights=nxt)
      else:
        w_q, w_ovo = _mla_fold(mla_w_uq[j], mla_w_ukv[j], mla_w_o[j])
        q, kv = _mla_proj(x, mod, i, norm_mix_g[i], positions,
                          mla_w_dq_dkv[j], mla_q_norm_g[j], mla_kv_norm_g[j],
                          w_q)
        attn_out, w_next = _mla_attention(q, kv, next_mlp_weights=nxt)
        attn = (attn_out, w_ovo)
      x = _mlp_layer(x, mod, i, norm_mlp_g[i], w1, w2, attn=attn, final_g=fg)
    if not last:
      w1, w2 = w_next
  return x
```

```python
import functools

import jax
import jax.numpy as jnp
from jax import lax
from jax.experimental import pallas as pl
from jax.experimental.pallas import tpu as pltpu

F32 = jnp.float32
BF16 = jnp.bfloat16

POOL_WINDOWS = (2, 4, 8, 16)
SGU_CHUNK = 128
SGU_HEAD_DIM = 128
MLA_HEADS = 16
MLA_Q_LORA = 256
MLA_KV_LORA = 128
MLA_NOPE = 128
MLA_ROPE = 64
MLA_V = 128
ROPE_THETA = 10000.0
RMS_EPS = 1e-6
LN_EPS = 1e-5
N_MIXERS = 3
MASK_VALUE = -1e30
LOG2_E = 1.4426950408889634
Q_SCALE = (MLA_NOPE + MLA_ROPE) ** -0.5 * LOG2_E

LANES = 128
VMEM_LIMIT_BYTES = 56 * 1024 * 1024

TOKEN_TILE = 512
MLP_TOKEN_TILE = 1024
SGU_TOKEN_TILE = 1024
PROJ_TOKEN_TILE = 1024
POOL_HALO = 16
ADA_COL_TILE = 2048
FOLD_HEADS_PER_STEP = 4
MLP_FF_TILE = 1024
ATTN_TILE = 512
ATTN_WIDE_BLOCKS = 2
QK_WIDTH = 2 * LANES


def _params(*semantics):
  return pltpu.CompilerParams(dimension_semantics=semantics,
                              vmem_limit_bytes=VMEM_LIMIT_BYTES)


def _const_spec(shape):
  zeros = (0,) * len(shape)
  return pl.BlockSpec(shape, lambda *_: zeros, pipeline_mode=pl.Buffered(1))


def _rms(x, eps):
  return x * lax.rsqrt(jnp.mean(x * x, axis=-1, keepdims=True) + eps)


def _norm_mod(x, g, shift, scale):
  return _rms(x, RMS_EPS) * g * (1.0 + scale) + shift


def _gelu(x):
  return 0.5 * x * (1.0 + lax.erf(x * (2.0 ** -0.5)))


def _staggered(stages_a, stages_b):
  n = len(stages_a)
  for k in range(n + 1):
    if k < n:
      stages_a[k]()
    if k >= 1:
      stages_b[k - 1]()


class _WeightCast:
  def __init__(self, items, n_steps, step_of):
    self.args = [w for w, _ in items]
    self.in_specs, self.out_specs, self.out_shape = [], [], []
    for w, layer in items:
      _, rows, cols = w.shape
      assert rows % n_steps == 0
      blk = rows // n_steps
      self.in_specs.append(pl.BlockSpec(
          (None, blk, cols),
          lambda *g, layer=layer: (layer, step_of(*g), 0)))
      self.out_specs.append(pl.BlockSpec(
          (blk, cols), lambda *g: (step_of(*g), 0)))
      self.out_shape.append(jax.ShapeDtypeStruct((rows, cols), BF16))

  @staticmethod
  def run(in_refs, out_refs):
    for src, dst in zip(in_refs, out_refs):
      dst[...] = src[...].astype(BF16)


def _ada_kernel(c_ref, w_ref, b_ref, o_ref):
  c = c_ref[...]
  ca = (c * jax.nn.sigmoid(c)).astype(BF16)
  o_ref[...] = jnp.dot(ca, w_ref[...].astype(BF16),
                       preferred_element_type=F32) + b_ref[...]


def _ada_mod(c, ada_w, ada_b):
  depth, d, d6 = ada_w.shape
  b = c.shape[0]
  tn = ADA_COL_TILE
  out = pl.pallas_call(
      _ada_kernel,
      grid=(depth, d6 // tn),
      in_specs=[
          pl.BlockSpec((b, d), lambda l, j: (0, 0)),
          pl.BlockSpec((None, d, tn), lambda l, j: (l, 0, j)),
          pl.BlockSpec((None, 1, tn), lambda l, j: (l, 0, j)),
      ],
      out_specs=pl.BlockSpec((None, b, tn), lambda l, j: (l, 0, j)),
      out_shape=jax.ShapeDtypeStruct((depth, b, d6), F32),
      compiler_params=_params("arbitrary", "arbitrary"),
      name="ada_mod",
  )(c, ada_w, ada_b.reshape(depth, 1, d6))
  return out.reshape(depth, b, 6, d)


def _pool_stages(x_ref, xh, i, g_ref, mod_ref, w_ref, ps_ref, a_scr, out_ref,
                 tm):
  def prep():
    g, shift, scale = g_ref[...], mod_ref[0:1, :], mod_ref[1:2, :]
    hh = _norm_mod(xh(), g, shift, scale)
    a_scr[0:POOL_HALO, :] = jnp.where(i > 0, hh, 0.0)
    a_scr[POOL_HALO:, :] = _norm_mod(x_ref[...], g, shift, scale)

  def group(gi, win):
    gdim = w_ref.shape[1]
    cols = slice(gi * gdim, (gi + 1) * gdim)
    s = a_scr[:, cols]
    k = 1
    while k < win:
      s = s + pltpu.roll(s, k, axis=0)
      k *= 2
    t = (i * tm + lax.broadcasted_iota(jnp.int32, (tm, 1), 0)).astype(F32)
    inv_count = 1.0 / jnp.minimum(t + 1.0, float(win))
    pooled = s[POOL_HALO:] * inv_count - a_scr[POOL_HALO:, cols]
    y = jnp.dot(pooled.astype(BF16), w_ref[gi], preferred_element_type=F32)
    out_ref[:, cols] = x_ref[:, cols] + mod_ref[2:3, cols] * (
        y * ps_ref[:, cols])

  return [prep] + [functools.partial(group, gi, win)
                   for gi, win in enumerate(POOL_WINDOWS)]


def _sgu_stages(rows, n_rows, x_ref, mod_ref, g_ref, win_ref, lng_ref, lnb_ref,
                ws_ref, bst_ref, wout_ref, o_ref, gated_scr):
  st = {}
  width = wout_ref.shape[0]
  t = SGU_CHUNK

  def project_v():
    st["h"] = _norm_mod(x_ref[rows, :], g_ref[...], mod_ref[0:1, :],
                        mod_ref[1:2, :]).astype(BF16)
    st["v"] = jnp.dot(st["h"], win_ref[:, width:], preferred_element_type=F32)

  def project_u():
    st["u"] = _gelu(jnp.dot(st["h"], win_ref[:, :width],
                            preferred_element_type=F32))

  def normalise():
    v = _gelu(st["v"])
    mu = jnp.mean(v, axis=-1, keepdims=True)
    vc = v - mu
    var = jnp.mean(vc * vc, axis=-1, keepdims=True)
    st["v"] = (vc * lax.rsqrt(var + LN_EPS) * lng_ref[...]
               + lnb_ref[...]).astype(BF16)

  def gate_spatial():
    u, v = st["u"], st["v"]
    causal = (lax.broadcasted_iota(jnp.int32, (t, t), 1)
              <= lax.broadcasted_iota(jnp.int32, (t, t), 0))
    for hd in range(ws_ref.shape[0]):
      cols = slice(hd * SGU_HEAD_DIM, (hd + 1) * SGU_HEAD_DIM)
      ws = jnp.where(causal, ws_ref[hd], 0.0).astype(BF16)
      bias = bst_ref[:, hd:hd + 1]
      for c in range(n_rows // t):
        blk = slice(c * t, (c + 1) * t)
        mixed = jnp.dot(ws, v[blk, cols], preferred_element_type=F32) + bias
        gated_scr[rows.start + c * t:rows.start + (c + 1) * t, cols] = (
            u[blk, cols] * mixed).astype(BF16)

  def project_out():
    y = jnp.dot(gated_scr[rows, :], wout_ref[...], preferred_element_type=F32)
    o_ref[rows, :] = x_ref[rows, :] + mod_ref[2:3, :] * y

  return [project_v, project_u, normalise, gate_spatial, project_out]


def _sgu_kernel(*refs, tm, n_cast):
  n_in = 9
  ins, cast_in = refs[:n_in], refs[n_in:n_in + n_cast]
  o_ref = refs[n_in + n_cast]
  cast_out = refs[n_in + n_cast + 1:n_in + 2 * n_cast + 1]
  gated_scr = refs[-1]
  _WeightCast.run(cast_in, cast_out)
  half = tm // 2
  _staggered(_sgu_stages(slice(0, half), half, *ins, o_ref, gated_scr),
             _sgu_stages(slice(half, tm), half, *ins, o_ref, gated_scr))


def _sgu_layer(x, mod, layer, g, w_in, ln_g, ln_b, w_s, b_s, w_out,
               side_casts=()):
  b, s, d = x.shape
  tm = SGU_TOKEN_TILE
  width = w_out.shape[0]
  heads, t, _ = w_s.shape
  assert t == SGU_CHUNK and tm % t == 0 and heads * SGU_HEAD_DIM == width
  n_i = s // tm
  in_specs = [
      pl.BlockSpec((None, tm, d), lambda bi, i: (bi, i, 0)),
      pl.BlockSpec((None, None, 6, d), lambda bi, i: (layer, bi, 0, 0)),
      _const_spec((1, d)),
      _const_spec((d, 2 * width)),
      _const_spec((1, width)),
      _const_spec((1, width)),
      _const_spec((heads, t, t)),
      _const_spec((t, heads)),
      _const_spec((width, d)),
  ]
  args = [x, mod, g.reshape(1, d), w_in, ln_g.reshape(1, width),
          ln_b.reshape(1, width), w_s, b_s.T, w_out]
  out_specs = [pl.BlockSpec((None, tm, d), lambda bi, i: (bi, i, 0))]
  out_shape = [jax.ShapeDtypeStruct(x.shape, F32)]
  n_cast = 0
  if side_casts:
    cast = _WeightCast(side_casts, b * n_i, lambda bi, i: bi * n_i + i)
    in_specs += cast.in_specs
    args += cast.args
    out_specs += cast.out_specs
    out_shape += cast.out_shape
    n_cast = len(cast.args)
  outs = pl.pallas_call(
      functools.partial(_sgu_kernel, tm=tm, n_cast=n_cast),
      grid=(b, n_i),
      in_specs=in_specs,
      out_specs=out_specs,
      out_shape=out_shape,
      scratch_shapes=[pltpu.VMEM((tm, width), BF16)],
      compiler_params=_params("arbitrary", "arbitrary"),
      name=f"sgu_mixer_{layer}",
  )(*args)
  return outs[0], tuple(outs[1:])


def _mla_fold_kernel(wqn_ref, wuk_ref, wuv_ref, wo_ref, qabs_ref, ovo_ref):
  kvl = wuv_ref.shape[1]
  for k in range(wqn_ref.shape[0]):
    qabs = lax.dot_general(
        wqn_ref[k], wuk_ref[k], (((1,), (1,)), ((), ())),
        precision=lax.Precision.HIGHEST, preferred_element_type=F32)
    qabs_ref[k] = (qabs * Q_SCALE).astype(BF16)
    ovo_ref[k * kvl:(k + 1) * kvl, :] = jnp.dot(
        wuv_ref[k], wo_ref[k], precision=lax.Precision.HIGHEST,
        preferred_element_type=F32).astype(BF16)


def _mla_fold(w_uq, w_ukv, w_o):
  h, ql, kvl = MLA_HEADS, MLA_Q_LORA, MLA_KV_LORA
  d = w_o.shape[1]
  w_uq3 = w_uq.reshape(ql, h, MLA_NOPE + MLA_ROPE)
  w_ukv3 = w_ukv.reshape(kvl, h, MLA_NOPE + MLA_V)
  wqn = w_uq3[:, :, :MLA_NOPE].transpose(1, 0, 2)
  wuk = w_ukv3[:, :, :MLA_NOPE].transpose(1, 0, 2)
  wuv = w_ukv3[:, :, MLA_NOPE:].transpose(1, 0, 2)
  wo = w_o.reshape(h, MLA_V, d)
  hb = FOLD_HEADS_PER_STEP
  assert h % hb == 0
  qabs, ovo = pl.pallas_call(
      _mla_fold_kernel,
      grid=(h // hb,),
      in_specs=[
          pl.BlockSpec((hb, ql, MLA_NOPE), lambda i: (i, 0, 0)),
          pl.BlockSpec((hb, kvl, MLA_NOPE), lambda i: (i, 0, 0)),
          pl.BlockSpec((hb, kvl, MLA_V), lambda i: (i, 0, 0)),
          pl.BlockSpec((hb, MLA_V, d), lambda i: (i, 0, 0)),
      ],
      out_specs=[
          pl.BlockSpec((hb, ql, kvl), lambda i: (i, 0, 0)),
          pl.BlockSpec((hb * kvl, d), lambda i: (i, 0)),
      ],
      out_shape=[
          jax.ShapeDtypeStruct((h, ql, kvl), BF16),
          jax.ShapeDtypeStruct((h * kvl, d), BF16),
      ],
      compiler_params=_params("arbitrary"),
      name="mla_fold",
  )(wqn, wuk, wuv, wo)
  half = MLA_ROPE // 2
  wr = w_uq3[:, :, MLA_NOPE:] * Q_SCALE
  wr_swapped = jnp.concatenate([wr[:, :, half:], wr[:, :, :half]], axis=-1)
  w_q = jnp.concatenate([
      qabs.transpose(1, 0, 2).reshape(ql, h * kvl),
      wr.reshape(ql, h * MLA_ROPE).astype(BF16),
      wr_swapped.reshape(ql, h * MLA_ROPE).astype(BF16),
  ], axis=1)
  return w_q, ovo


def _mla_proj_stages(rows, x_ref, mod_ref, g_ref, pos_ref, wlat_ref, qg_ref,
                     kvg_ref, wq_ref, freq_ref, sign_ref, q_ref, kv_ref):
  st = {}
  ql, kvl = MLA_Q_LORA, MLA_KV_LORA
  n = rows.stop - rows.start
  mid = rows.start + n // 2
  heads = q_ref.shape[0]
  hw = heads * LANES

  def project_latent():
    h = _norm_mod(x_ref[rows, :], g_ref[...], mod_ref[0:1, :],
                  mod_ref[1:2, :]).astype(BF16)
    st["lat"] = jnp.dot(h, wlat_ref[...], preferred_element_type=F32)

  def rotary_and_keys():
    lat = st["lat"]
    st["cq"] = (_rms(lat[:, :ql], RMS_EPS) * qg_ref[...]).astype(BF16)
    ckv = _rms(lat[:, ql:ql + kvl], RMS_EPS) * kvg_ref[...]
    low = lax.broadcasted_iota(jnp.int32, (1, LANES), 1) < LANES // 2
    pos = jnp.where(low, pos_ref[rows.start:mid, :], pos_ref[mid:rows.stop, :])
    ang = pos.astype(F32) * freq_ref[...]

    def spread(t):
      r = pltpu.roll(t, LANES // 2, axis=1)
      return jnp.concatenate([jnp.where(low, t, r), jnp.where(low, r, t)],
                             axis=0)

    cos = spread(jnp.cos(ang))
    sin = spread(jnp.sin(ang)) * sign_ref[...]
    kra = lat[:, ql + kvl:ql + kvl + LANES]
    krb = lat[:, ql + kvl + LANES:]
    kv_ref[rows, :LANES] = ckv.astype(BF16)
    kv_ref[rows, LANES:] = (kra * cos + krb * sin).astype(BF16)
    st["cos2"] = jnp.concatenate([cos, cos], axis=1)
    st["sin2"] = jnp.concatenate([sin, sin], axis=1)

  def queries(quads):
    cq = st["cq"]
    hr = heads * MLA_ROPE
    for u in quads:
      r0 = hw + 2 * u * LANES
      ra = jnp.dot(cq, wq_ref[:, r0:r0 + 2 * LANES],
                   preferred_element_type=F32)
      rb = jnp.dot(cq, wq_ref[:, hr + r0:hr + r0 + 2 * LANES],
                   preferred_element_type=F32)
      qr = ra * st["cos2"] + rb * st["sin2"]
      for v in range(2):
        c0 = (4 * u + 2 * v) * LANES
        qa = jnp.dot(cq, wq_ref[:, c0:c0 + 2 * LANES],
                     preferred_element_type=F32)
        pair = qr[:, v * LANES:(v + 1) * LANES]
        tiles = (pair, pltpu.roll(pair, LANES // 2, axis=1))
        for k in range(2):
          hd = 4 * u + 2 * v + k
          q_ref[hd, rows, :LANES] = qa[:, k * LANES:(k + 1) * LANES].astype(BF16)
          q_ref[hd, rows, LANES:] = tiles[k].astype(BF16)

  n_quads = heads // 4
  return [project_latent, rotary_and_keys,
          functools.partial(queries, range(0, n_quads // 2)),
          functools.partial(queries, range(n_quads // 2, n_quads))]


def _mla_proj_kernel(*refs, tm):
  half = tm // 2
  _staggered(_mla_proj_stages(slice(0, half), *refs),
             _mla_proj_stages(slice(half, tm), *refs))


def _mla_proj(x, mod, layer, g, positions, w_dq_dkv, q_norm_g, kv_norm_g, w_q):
  b, s, d = x.shape
  tm = PROJ_TOKEN_TILE
  ql, kvl, rope, h = MLA_Q_LORA, MLA_KV_LORA, MLA_ROPE, MLA_HEADS
  half = rope // 2
  assert 2 * rope == LANES and h % 4 == 0
  wkr = w_dq_dkv[:, ql + kvl:]
  zeros = jnp.zeros((d, LANES - rope), F32)
  w_lat = jnp.concatenate(
      [w_dq_dkv[:, :ql + kvl], wkr, zeros, wkr[:, half:], wkr[:, :half], zeros],
      axis=1).astype(BF16)
  inv_freq = ROPE_THETA ** (-jnp.arange(0, rope, 2, dtype=F32) / rope)
  freq = jnp.tile(inv_freq, LANES // half).reshape(1, LANES)
  sign = jnp.tile(jnp.concatenate([-jnp.ones((half,), F32),
                                   jnp.ones((half,), F32)]),
                  LANES // rope).reshape(1, LANES)
  return pl.pallas_call(
      functools.partial(_mla_proj_kernel, tm=tm),
      grid=(b, s // tm),
      in_specs=[
          pl.BlockSpec((None, tm, d), lambda bi, i: (bi, i, 0)),
          pl.BlockSpec((None, None, 6, d), lambda bi, i: (layer, bi, 0, 0)),
          _const_spec((1, d)),
          pl.BlockSpec((None, tm, 1), lambda bi, i: (bi, i, 0)),
          _const_spec(w_lat.shape),
          _const_spec((1, ql)),
          _const_spec((1, kvl)),
          _const_spec(w_q.shape),
          _const_spec((1, LANES)),
          _const_spec((1, LANES)),
      ],
      out_specs=[
          pl.BlockSpec((None, h, tm, QK_WIDTH), lambda bi, i: (bi, 0, i, 0)),
          pl.BlockSpec((None, tm, QK_WIDTH), lambda bi, i: (bi, i, 0)),
      ],
      out_shape=[
          jax.ShapeDtypeStruct((b, h, s, QK_WIDTH), BF16),
          jax.ShapeDtypeStruct((b, s, QK_WIDTH), BF16),
      ],
      compiler_params=_params("arbitrary", "arbitrary"),
      name=f"mla_proj_{layer}",
  )(x, mod, g.reshape(1, d), positions.reshape(b, s, 1), w_lat,
    q_norm_g.reshape(1, ql), kv_norm_g.reshape(1, kvl), w_q, freq, sign)


def _flash_kernel(*refs, tq, n_cast):
  q_ref, kv_ref = refs[:2]
  cast_in = refs[2:2 + n_cast]
  o_ref = refs[2 + n_cast]
  cast_out = refs[3 + n_cast:3 + 2 * n_cast]
  m_scr, l_scr, acc_scr = refs[-3:]
  _WeightCast.run(cast_in, cast_out)
  i = pl.program_id(1)
  heads = q_ref.shape[0]

  def block(key_start, n_keys, row_lo=0, row_hi=tq, masked=False, first=False):
    n_rows = row_hi - row_lo
    kv = kv_ref[pl.ds(key_start, n_keys), :]
    v_ones = jnp.concatenate([kv[:, :LANES], jnp.ones((n_keys, LANES), BF16)],
                             axis=1)
    s, p, alpha = {}, {}, {}
    if masked:
      visible = (lax.broadcasted_iota(jnp.int32, (n_rows, n_keys), 1)
                 <= lax.broadcasted_iota(jnp.int32, (n_rows, n_keys), 0))

    def rows_of(hd):
      return slice(hd * tq + row_lo, hd * tq + row_hi)

    def scores(hd):
      s[hd] = lax.dot_general(q_ref[hd, row_lo:row_hi, :], kv,
                              (((1,), (1,)), ((), ())),
                              preferred_element_type=F32)

    def softmax(hd):
      rows = rows_of(hd)
      sg = s.pop(hd)
      if masked:
        sg = jnp.where(visible, sg, MASK_VALUE)
      m_cur = jnp.max(sg, axis=1, keepdims=True)
      if first:
        m_new = jnp.broadcast_to(m_cur, (n_rows, LANES))
      else:
        m_prev = m_scr[rows]
        m_new = jnp.maximum(m_prev, m_cur)
        alpha[hd] = jnp.exp2(m_prev - m_new)
      ps = [jnp.exp2(sg[:, k * LANES:(k + 1) * LANES] - m_new)
            for k in range(n_keys // LANES)]
      p[hd] = jnp.concatenate(ps, axis=1).astype(BF16)
      m_scr[rows] = m_new

    def weighted_values(hd):
      rows = rows_of(hd)
      pv = jnp.dot(p.pop(hd), v_ones, preferred_element_type=F32)
      if first:
        acc_scr[rows] = pv[:, :LANES]
        l_scr[rows] = pv[:, LANES:]
      else:
        a = alpha.pop(hd)
        acc_scr[rows] = a * acc_scr[rows] + pv[:, :LANES]
        l_scr[rows] = a * l_scr[rows] + pv[:, LANES:]

    scores(0)
    for hd in range(heads):
      if hd + 1 < heads:
        scores(hd + 1)
      softmax(hd)
      weighted_values(hd)

  diag = pl.multiple_of(i * tq, tq)
  half = tq // 2
  block(diag, half, masked=True, first=True)
  block(diag + half, half, row_lo=half, masked=True)
  wide = ATTN_WIDE_BLOCKS * tq

  def wide_block(j, carry):
    block(pl.multiple_of(j * wide, wide), wide)
    return carry

  n_wide = i // ATTN_WIDE_BLOCKS
  lax.fori_loop(0, n_wide, wide_block, 0)
  for r in range(ATTN_WIDE_BLOCKS - 1):
    @pl.when(i - n_wide * ATTN_WIDE_BLOCKS > r)
    def _(r=r):
      block(pl.multiple_of((n_wide * ATTN_WIDE_BLOCKS + r) * tq, tq), tq)
  for hd in range(heads):
    rows = slice(hd * tq, (hd + 1) * tq)
    o_ref[:, hd * LANES:(hd + 1) * LANES] = (
        acc_scr[rows] / l_scr[rows]).astype(BF16)


def _mla_attention(q, kv, side_casts=()):
  b, h, s, w = q.shape
  tq = ATTN_TILE
  rows = h * tq
  n_i = s // tq
  in_specs = [
      pl.BlockSpec((None, h, tq, w), lambda bi, i: (bi, 0, i, 0)),
      pl.BlockSpec((None, s, w), lambda bi, i: (bi, 0, 0)),
  ]
  args = [q, kv]
  out_specs = [pl.BlockSpec((None, tq, h * LANES), lambda bi, i: (bi, i, 0))]
  out_shape = [jax.ShapeDtypeStruct((b, s, h * LANES), BF16)]
  n_cast = 0
  if side_casts:
    cast = _WeightCast(side_casts, b * n_i, lambda bi, i: bi * n_i + i)
    in_specs += cast.in_specs
    args += cast.args
    out_specs += cast.out_specs
    out_shape += cast.out_shape
    n_cast = len(cast.args)
  outs = pl.pallas_call(
      functools.partial(_flash_kernel, tq=tq, n_cast=n_cast),
      grid=(b, n_i),
      in_specs=in_specs,
      out_specs=out_specs,
      out_shape=out_shape,
      scratch_shapes=[pltpu.VMEM((rows, LANES), F32)] * 3,
      compiler_params=_params("arbitrary", "arbitrary"),
      name="mla_attention",
  )(*args)
  return outs[0], tuple(outs[1:])


def _mlp_core(x, mod_ref, g_ref, w1_ref, w2_ref, fg_ref, h_scr, acc_scr, tf,
              side_work=()):
  h_scr[...] = _norm_mod(x, g_ref[...], mod_ref[3:4, :],
                         mod_ref[4:5, :]).astype(BF16)
  d_ff = w1_ref.shape[1]
  n_chunks = d_ff // tf
  side_work = list(side_work)
  for j in range(n_chunks):
    cols = slice(j * tf, (j + 1) * tf)
    a = jnp.dot(h_scr[...], w1_ref[:, cols], preferred_element_type=F32)
    a = jnp.maximum(a, 0.0)
    part = jnp.dot((a * a).astype(BF16), w2_ref[cols, :],
                   preferred_element_type=F32)
    if j == 0:
      acc_scr[...] = part
    else:
      acc_scr[...] += part
    slots = max(n_chunks - 1, 1)
    lo = min(-(-j * len(side_work) // slots), len(side_work))
    hi = min(-(-(j + 1) * len(side_work) // slots), len(side_work))
    for work in side_work[lo:hi]:
      work()
  out = x + mod_ref[5:6, :] * acc_scr[...]
  if fg_ref is not None:
    out = _rms(out, RMS_EPS) * fg_ref[...]
  return out


def _mlp_stages(rows, get_x, mod_ref, g_ref, w1_ref, w2_ref, fg_ref, o_ref,
                h_scr, acc_scr, tf):
  def normalise():
    x = get_x()
    o_ref[rows, :] = x
    h_scr[rows, :] = _norm_mod(x, g_ref[...], mod_ref[3:4, :],
                               mod_ref[4:5, :]).astype(BF16)

  def chunk(j):
    cols = slice(j * tf, (j + 1) * tf)
    a = jnp.dot(h_scr[rows, :], w1_ref[:, cols], preferred_element_type=F32)
    a = jnp.maximum(a, 0.0)
    part = jnp.dot((a * a).astype(BF16), w2_ref[cols, :],
                   preferred_element_type=F32)
    if j == 0:
      acc_scr[rows, :] = part
    else:
      acc_scr[rows, :] += part

  def finish():
    out = o_ref[rows, :] + mod_ref[5:6, :] * acc_scr[rows, :]
    if fg_ref is not None:
      out = _rms(out, RMS_EPS) * fg_ref[...]
    o_ref[rows, :] = out

  n_chunks = w1_ref.shape[1] // tf
  return ([normalise] + [functools.partial(chunk, j) for j in range(n_chunks)]
          + [finish])


def _mlp_kernel(*refs, tm, tf, with_attn, final):
  refs = list(refs)
  x_ref, mod_ref, g_ref, w1_ref, w2_ref = refs[:5]
  rest = refs[5:]
  if with_attn:
    attn_ref, wo_ref = rest[:2]
    rest = rest[2:]
  fg_ref = None
  if final:
    fg_ref = rest[0]
    rest = rest[1:]
  o_ref, h_scr, acc_scr = rest

  def stages(rows):
    def get_x():
      x = x_ref[rows, :]
      if with_attn:
        y = jnp.dot(attn_ref[rows, :], wo_ref[...], preferred_element_type=F32)
        x = x + mod_ref[2:3, :] * y
      return x
    return _mlp_stages(rows, get_x, mod_ref, g_ref, w1_ref, w2_ref, fg_ref,
                       o_ref, h_scr, acc_scr, tf)

  half = tm // 2
  _staggered(stages(slice(0, half)), stages(slice(half, tm)))


def _mlp_layer(x, mod, layer, g, w1, w2, attn=None, final_g=None):
  b, s, d = x.shape
  tm = MLP_TOKEN_TILE
  d_ff = w1.shape[-1]
  with_attn = attn is not None
  final = final_g is not None
  in_specs = [
      pl.BlockSpec((None, tm, d), lambda bi, i: (bi, i, 0)),
      pl.BlockSpec((None, None, 6, d), lambda bi, i: (layer, bi, 0, 0)),
      _const_spec((1, d)),
      _const_spec((d, d_ff)),
      _const_spec((d_ff, d)),
  ]
  args = [x, mod, g.reshape(1, d), w1, w2]
  if with_attn:
    attn_out, w_ovo = attn
    in_specs += [
        pl.BlockSpec((None, tm, attn_out.shape[-1]), lambda bi, i: (bi, i, 0)),
        _const_spec(w_ovo.shape),
    ]
    args += [attn_out, w_ovo]
  if final:
    in_specs.append(_const_spec((1, d)))
    args.append(final_g.reshape(1, d))
  return pl.pallas_call(
      functools.partial(_mlp_kernel, tm=tm, tf=MLP_FF_TILE,
                        with_attn=with_attn, final=final),
      grid=(b, s // tm),
      in_specs=in_specs,
      out_specs=pl.BlockSpec((None, tm, d), lambda bi, i: (bi, i, 0)),
      out_shape=jax.ShapeDtypeStruct(x.shape, F32),
      scratch_shapes=[pltpu.VMEM((tm, d), BF16), pltpu.VMEM((tm, d), F32)],
      compiler_params=_params("arbitrary", "arbitrary"),
      name=f"mlp_{layer}",
  )(*args)


def _pool_mlp_kernel(*refs, tm, tf, tiles_per_seq, final, n_cast):
  refs = list(refs)
  (x0_ref, xn_ref, xhn_ref, mod_ref, modn_ref, gmix_ref, pw_ref, ps_ref,
   g_ref, w1_ref, w2_ref) = refs[:11]
  rest = refs[11:]
  fg_ref = None
  if final:
    fg_ref = rest[0]
    rest = rest[1:]
  cast_in, rest = rest[:n_cast], rest[n_cast:]
  o_ref, rest = rest[0], rest[1:]
  cast_out, rest = rest[:n_cast], rest[n_cast:]
  x1_scr, a_scr, h_scr, acc_scr = rest
  _WeightCast.run(cast_in, cast_out)
  t = pl.program_id(0)
  slot = t % 2

  @pl.when(t == 0)
  def _():
    for stage in _pool_stages(x0_ref, lambda: x0_ref[0:POOL_HALO, :], 0,
                              gmix_ref, mod_ref, pw_ref, ps_ref, a_scr,
                              x1_scr.at[0], tm):
      stage()

  next_stages = _pool_stages(xn_ref, lambda: xhn_ref[...],
                             (t + 1) % tiles_per_seq, gmix_ref, modn_ref,
                             pw_ref, ps_ref, a_scr, x1_scr.at[1 - slot], tm)
  o_ref[...] = _mlp_core(x1_scr[slot], mod_ref, g_ref, w1_ref, w2_ref, fg_ref,
                         h_scr, acc_scr, tf, side_work=next_stages)


def _pool_mlp_layer(x, mod, layer, g_mix, pool_w, pool_scale, g, w1, w2,
                    final_g=None, side_casts=()):
  b, s, d = x.shape
  tm = TOKEN_TILE
  d_ff = w1.shape[-1]
  assert POOL_HALO >= max(POOL_WINDOWS) and tm % POOL_HALO == 0 and s % tm == 0
  tiles_per_seq = s // tm
  n_tiles = b * tiles_per_seq
  halo_per_tile = tm // POOL_HALO
  final = final_g is not None
  xf = x.reshape(b * s, d)

  def nxt(t):
    return jnp.minimum(t + 1, n_tiles - 1)

  in_specs = [
      pl.BlockSpec((tm, d), lambda t: (0, 0), pipeline_mode=pl.Buffered(1)),
      pl.BlockSpec((tm, d), lambda t: (nxt(t), 0)),
      pl.BlockSpec((POOL_HALO, d),
                   lambda t: (jnp.maximum(nxt(t) * halo_per_tile - 1, 0), 0)),
      pl.BlockSpec((None, None, 6, d),
                   lambda t: (layer, t // tiles_per_seq, 0, 0)),
      pl.BlockSpec((None, None, 6, d),
                   lambda t: (layer, nxt(t) // tiles_per_seq, 0, 0)),
      _const_spec((1, d)),
      _const_spec(pool_w.shape),
      _const_spec((1, d)),
      _const_spec((1, d)),
      _const_spec((d, d_ff)),
      _const_spec((d_ff, d)),
  ]
  args = [xf, xf, xf, mod, mod, g_mix.reshape(1, d), pool_w.astype(BF16),
          pool_scale.reshape(1, d), g.reshape(1, d), w1, w2]
  if final:
    in_specs.append(_const_spec((1, d)))
    args.append(final_g.reshape(1, d))
  out_specs = [pl.BlockSpec((tm, d), lambda t: (t, 0))]
  out_shape = [jax.ShapeDtypeStruct(xf.shape, F32)]
  n_cast = 0
  if side_casts:
    cast = _WeightCast(side_casts, n_tiles, lambda t: t)
    in_specs += cast.in_specs
    args += cast.args
    out_specs += cast.out_specs
    out_shape += cast.out_shape
    n_cast = len(cast.args)
  outs = pl.pallas_call(
      functools.partial(_pool_mlp_kernel, tm=tm, tf=MLP_FF_TILE,
                        tiles_per_seq=tiles_per_seq, final=final,
                        n_cast=n_cast),
      grid=(n_tiles,),
      in_specs=in_specs,
      out_specs=out_specs,
      out_shape=out_shape,
      scratch_shapes=[pltpu.VMEM((2, tm, d), F32),
                      pltpu.VMEM((POOL_HALO + tm, d), F32),
                      pltpu.VMEM((tm, d), BF16), pltpu.VMEM((tm, d), F32)],
      compiler_params=_params("arbitrary"),
      name=f"pool_mlp_{layer}",
  )(*args)
  return outs[0].reshape(b, s, d), tuple(outs[1:])


def kernel(x, c, positions, ada_w, ada_b, norm_mix_g, norm_mlp_g, pool_w, pool_scale, sgu_w_in, sgu_ln_g, sgu_ln_b, sgu_w_s, sgu_b_s, sgu_w_out, mla_w_dq_dkv, mla_q_norm_g, mla_kv_norm_g, mla_w_uq, mla_w_ukv, mla_w_o, mlp_w1, mlp_w2, final_g):
  depth = ada_w.shape[0]
  mod = _ada_mod(c, ada_w, ada_b)
  def big_weights(i):
    items = [(mlp_w1, i), (mlp_w2, i)]
    if i % N_MIXERS == 1:
      items += [(sgu_w_in, i // N_MIXERS), (sgu_w_out, i // N_MIXERS)]
    return items

  weights = [w[l].astype(BF16) for w, l in big_weights(0)]
  for i in range(depth):
    kind, j = i % N_MIXERS, i // N_MIXERS
    last = i == depth - 1
    fg = final_g if last else None
    side = () if last else big_weights(i + 1)
    w1, w2 = weights[:2]
    if kind == 0:
      x, weights = _pool_mlp_layer(
          x, mod, i, norm_mix_g[i], pool_w[j], pool_scale[j], norm_mlp_g[i],
          w1, w2, final_g=fg, side_casts=side)
    else:
      attn = None
      if kind == 1:
        x, weights = _sgu_layer(
            x, mod, i, norm_mix_g[i], weights[2], sgu_ln_g[j], sgu_ln_b[j],
            sgu_w_s[j], sgu_b_s[j], weights[3], side_casts=side)
      else:
        w_q, w_ovo = _mla_fold(mla_w_uq[j], mla_w_ukv[j], mla_w_o[j])
        q, kv = _mla_proj(x, mod, i, norm_mix_g[i], positions,
                          mla_w_dq_dkv[j], mla_q_norm_g[j], mla_kv_norm_g[j],
                          w_q)
        attn_out, weights = _mla_attention(q, kv, side_casts=side)
        attn = (attn_out, w_ovo)
      x = _mlp_layer(x, mod, i, norm_mlp_g[i], w1, w2, attn=attn, final_g=fg)
  return x
```

```python
import functools

import jax
import jax.numpy as jnp
from jax import lax
from jax.experimental import pallas as pl
from jax.experimental.pallas import tpu as pltpu

F32 = jnp.float32
BF16 = jnp.bfloat16

POOL_WINDOWS = (2, 4, 8, 16)
SGU_CHUNK = 128
SGU_HEAD_DIM = 128
MLA_HEADS = 16
MLA_Q_LORA = 256
MLA_KV_LORA = 128
MLA_NOPE = 128
MLA_ROPE = 64
MLA_V = 128
ROPE_THETA = 10000.0
RMS_EPS = 1e-6
LN_EPS = 1e-5
N_MIXERS = 3
MASK_VALUE = -1e30
LOG2_E = 1.4426950408889634
Q_SCALE = (MLA_NOPE + MLA_ROPE) ** -0.5 * LOG2_E

LANES = 128
VMEM_LIMIT_BYTES = 56 * 1024 * 1024

TOKEN_TILE = 512
MLP_TOKEN_TILE = 1024
SGU_TOKEN_TILE = 1024
PROJ_TOKEN_TILE = 1024
POOL_HALO = 16
ADA_COL_TILE = 2048
FOLD_HEADS_PER_STEP = 4
MLP_FF_TILE = 1024
ATTN_TILE = 512
ATTN_WIDE_BLOCKS = 2
QK_WIDTH = 2 * LANES


def _params(*semantics, fuse_inputs=0):
  return pltpu.CompilerParams(
      dimension_semantics=semantics, vmem_limit_bytes=VMEM_LIMIT_BYTES,
      allow_input_fusion=[True] * fuse_inputs if fuse_inputs else None)


def _const_spec(shape):
  zeros = (0,) * len(shape)
  return pl.BlockSpec(shape, lambda *_: zeros, pipeline_mode=pl.Buffered(1))


def _rms(x, eps):
  return x * lax.rsqrt(jnp.mean(x * x, axis=-1, keepdims=True) + eps)


def _norm_mod(x, g, shift, scale):
  return _rms(x, RMS_EPS) * g * (1.0 + scale) + shift


def _gelu(x):
  return 0.5 * x * (1.0 + lax.erf(x * (2.0 ** -0.5)))


def _staggered(stages_a, stages_b):
  n = len(stages_a)
  for k in range(n + 1):
    if k < n:
      stages_a[k]()
    if k >= 1:
      stages_b[k - 1]()


class _WeightCast:
  def __init__(self, items, n_steps, step_of):
    self.args = [w for w, _ in items]
    self.in_specs, self.out_specs, self.out_shape = [], [], []
    for w, layer in items:
      _, rows, cols = w.shape
      assert rows % n_steps == 0
      blk = rows // n_steps
      self.in_specs.append(pl.BlockSpec(
          (None, blk, cols),
          lambda *g, layer=layer: (layer, step_of(*g), 0)))
      self.out_specs.append(pl.BlockSpec(
          (blk, cols), lambda *g: (step_of(*g), 0)))
      self.out_shape.append(jax.ShapeDtypeStruct((rows, cols), BF16))

  @staticmethod
  def run(in_refs, out_refs):
    for src, dst in zip(in_refs, out_refs):
      dst[...] = src[...].astype(BF16)


def _ada_kernel(c_ref, w_ref, b_ref, o_ref):
  c = c_ref[...]
  ca = (c * jax.nn.sigmoid(c)).astype(BF16)
  o_ref[...] = jnp.dot(ca, w_ref[...].astype(BF16),
                       preferred_element_type=F32) + b_ref[...]


def _ada_mod(c, ada_w, ada_b):
  depth, d, d6 = ada_w.shape
  b = c.shape[0]
  tn = ADA_COL_TILE
  out = pl.pallas_call(
      _ada_kernel,
      grid=(depth, d6 // tn),
      in_specs=[
          pl.BlockSpec((b, d), lambda l, j: (0, 0)),
          pl.BlockSpec((None, d, tn), lambda l, j: (l, 0, j)),
          pl.BlockSpec((None, 1, tn), lambda l, j: (l, 0, j)),
      ],
      out_specs=pl.BlockSpec((None, b, tn), lambda l, j: (l, 0, j)),
      out_shape=jax.ShapeDtypeStruct((depth, b, d6), F32),
      compiler_params=_params("arbitrary", "arbitrary"),
      name="ada_mod",
  )(c, ada_w, ada_b.reshape(depth, 1, d6))
  return out.reshape(depth, b, 6, d)


def _pool_stages(x_ref, xh, i, g_ref, mod_ref, w_ref, ps_ref, a_scr, out_ref,
                 tm):
  def prep():
    g, shift, scale = g_ref[...], mod_ref[0:1, :], mod_ref[1:2, :]
    hh = _norm_mod(xh(), g, shift, scale)
    a_scr[0:POOL_HALO, :] = jnp.where(i > 0, hh, 0.0)
    a_scr[POOL_HALO:, :] = _norm_mod(x_ref[...], g, shift, scale)

  def group(gi, win):
    gdim = w_ref.shape[1]
    cols = slice(gi * gdim, (gi + 1) * gdim)
    s = a_scr[:, cols]
    k = 1
    while k < win:
      s = s + pltpu.roll(s, k, axis=0)
      k *= 2
    t = (i * tm + lax.broadcasted_iota(jnp.int32, (tm, 1), 0)).astype(F32)
    inv_count = 1.0 / jnp.minimum(t + 1.0, float(win))
    pooled = s[POOL_HALO:] * inv_count - a_scr[POOL_HALO:, cols]
    y = jnp.dot(pooled.astype(BF16), w_ref[gi], preferred_element_type=F32)
    out_ref[:, cols] = x_ref[:, cols] + mod_ref[2:3, cols] * (
        y * ps_ref[:, cols])

  return [prep] + [functools.partial(group, gi, win)
                   for gi, win in enumerate(POOL_WINDOWS)]


def _sgu_stages(rows, n_rows, x_ref, mod_ref, g_ref, win_ref, lng_ref, lnb_ref,
                ws_ref, bst_ref, wout_ref, o_ref, gated_scr):
  st = {}
  width = wout_ref.shape[0]
  t = SGU_CHUNK

  def project_v():
    st["h"] = _norm_mod(x_ref[rows, :], g_ref[...], mod_ref[0:1, :],
                        mod_ref[1:2, :]).astype(BF16)
    st["v"] = jnp.dot(st["h"], win_ref[:, width:], preferred_element_type=F32)

  def project_u():
    st["u"] = _gelu(jnp.dot(st["h"], win_ref[:, :width],
                            preferred_element_type=F32))

  def normalise():
    v = _gelu(st["v"])
    mu = jnp.mean(v, axis=-1, keepdims=True)
    vc = v - mu
    var = jnp.mean(vc * vc, axis=-1, keepdims=True)
    st["v"] = (vc * lax.rsqrt(var + LN_EPS) * lng_ref[...]
               + lnb_ref[...]).astype(BF16)

  def gate_spatial():
    u, v = st["u"], st["v"]
    causal = (lax.broadcasted_iota(jnp.int32, (t, t), 1)
              <= lax.broadcasted_iota(jnp.int32, (t, t), 0))
    for hd in range(ws_ref.shape[0]):
      cols = slice(hd * SGU_HEAD_DIM, (hd + 1) * SGU_HEAD_DIM)
      ws = jnp.where(causal, ws_ref[hd], 0.0).astype(BF16)
      bias = bst_ref[:, hd:hd + 1]
      for c in range(n_rows // t):
        blk = slice(c * t, (c + 1) * t)
        mixed = jnp.dot(ws, v[blk, cols], preferred_element_type=F32) + bias
        gated_scr[rows.start + c * t:rows.start + (c + 1) * t, cols] = (
            u[blk, cols] * mixed).astype(BF16)

  def project_out():
    y = jnp.dot(gated_scr[rows, :], wout_ref[...], preferred_element_type=F32)
    o_ref[rows, :] = x_ref[rows, :] + mod_ref[2:3, :] * y

  return [project_v, project_u, normalise, gate_spatial, project_out]


def _sgu_kernel(*refs, tm, n_cast):
  n_in = 9
  ins, cast_in = refs[:n_in], refs[n_in:n_in + n_cast]
  o_ref = refs[n_in + n_cast]
  cast_out = refs[n_in + n_cast + 1:n_in + 2 * n_cast + 1]
  gated_scr = refs[-1]
  _WeightCast.run(cast_in, cast_out)
  half = tm // 2
  _staggered(_sgu_stages(slice(0, half), half, *ins, o_ref, gated_scr),
             _sgu_stages(slice(half, tm), half, *ins, o_ref, gated_scr))


def _sgu_layer(x, mod, layer, g, w_in, ln_g, ln_b, w_s, b_s, w_out,
               side_casts=()):
  b, s, d = x.shape
  tm = SGU_TOKEN_TILE
  width = w_out.shape[0]
  heads, t, _ = w_s.shape
  assert t == SGU_CHUNK and tm % t == 0 and heads * SGU_HEAD_DIM == width
  n_i = s // tm
  in_specs = [
      pl.BlockSpec((None, tm, d), lambda bi, i: (bi, i, 0)),
      pl.BlockSpec((None, None, 6, d), lambda bi, i: (layer, bi, 0, 0)),
      _const_spec((1, d)),
      _const_spec((d, 2 * width)),
      _const_spec((1, width)),
      _const_spec((1, width)),
      _const_spec((heads, t, t)),
      _const_spec((t, heads)),
      _const_spec((width, d)),
  ]
  args = [x, mod, g.reshape(1, d), w_in, ln_g.reshape(1, width),
          ln_b.reshape(1, width), w_s, b_s.T, w_out]
  out_specs = [pl.BlockSpec((None, tm, d), lambda bi, i: (bi, i, 0))]
  out_shape = [jax.ShapeDtypeStruct(x.shape, F32)]
  n_cast = 0
  if side_casts:
    cast = _WeightCast(side_casts, b * n_i, lambda bi, i: bi * n_i + i)
    in_specs += cast.in_specs
    args += cast.args
    out_specs += cast.out_specs
    out_shape += cast.out_shape
    n_cast = len(cast.args)
  outs = pl.pallas_call(
      functools.partial(_sgu_kernel, tm=tm, n_cast=n_cast),
      grid=(b, n_i),
      in_specs=in_specs,
      out_specs=out_specs,
      out_shape=out_shape,
      scratch_shapes=[pltpu.VMEM((tm, width), BF16)],
      compiler_params=_params("arbitrary", "arbitrary", fuse_inputs=len(args)),
      name=f"sgu_mixer_{layer}",
  )(*args)
  return outs[0], tuple(outs[1:])


def _mla_fold_kernel(wqn_ref, wuk_ref, wuv_ref, wo_ref, qabs_ref, ovo_ref):
  kvl = wuv_ref.shape[1]
  for k in range(wqn_ref.shape[0]):
    qabs = lax.dot_general(
        wqn_ref[k], wuk_ref[k], (((1,), (1,)), ((), ())),
        precision=lax.Precision.HIGHEST, preferred_element_type=F32)
    qabs_ref[k] = (qabs * Q_SCALE).astype(BF16)
    ovo_ref[k * kvl:(k + 1) * kvl, :] = jnp.dot(
        wuv_ref[k], wo_ref[k], precision=lax.Precision.HIGHEST,
        preferred_element_type=F32).astype(BF16)


def _mla_fold(w_uq, w_ukv, w_o):
  h, ql, kvl = MLA_HEADS, MLA_Q_LORA, MLA_KV_LORA
  d = w_o.shape[1]
  w_uq3 = w_uq.reshape(ql, h, MLA_NOPE + MLA_ROPE)
  w_ukv3 = w_ukv.reshape(kvl, h, MLA_NOPE + MLA_V)
  wqn = w_uq3[:, :, :MLA_NOPE].transpose(1, 0, 2)
  wuk = w_ukv3[:, :, :MLA_NOPE].transpose(1, 0, 2)
  wuv = w_ukv3[:, :, MLA_NOPE:].transpose(1, 0, 2)
  wo = w_o.reshape(h, MLA_V, d)
  hb = FOLD_HEADS_PER_STEP
  assert h % hb == 0
  qabs, ovo = pl.pallas_call(
      _mla_fold_kernel,
      grid=(h // hb,),
      in_specs=[
          pl.BlockSpec((hb, ql, MLA_NOPE), lambda i: (i, 0, 0)),
          pl.BlockSpec((hb, kvl, MLA_NOPE), lambda i: (i, 0, 0)),
          pl.BlockSpec((hb, kvl, MLA_V), lambda i: (i, 0, 0)),
          pl.BlockSpec((hb, MLA_V, d), lambda i: (i, 0, 0)),
      ],
      out_specs=[
          pl.BlockSpec((hb, ql, kvl), lambda i: (i, 0, 0)),
          pl.BlockSpec((hb * kvl, d), lambda i: (i, 0)),
      ],
      out_shape=[
          jax.ShapeDtypeStruct((h, ql, kvl), BF16),
          jax.ShapeDtypeStruct((h * kvl, d), BF16),
      ],
      compiler_params=_params("arbitrary", fuse_inputs=4),
      name="mla_fold",
  )(wqn, wuk, wuv, wo)
  half = MLA_ROPE // 2
  wr = w_uq3[:, :, MLA_NOPE:] * Q_SCALE
  wr_swapped = jnp.concatenate([wr[:, :, half:], wr[:, :, :half]], axis=-1)
  w_q = jnp.concatenate([
      qabs.transpose(1, 0, 2).reshape(ql, h * kvl),
      wr.reshape(ql, h * MLA_ROPE).astype(BF16),
      wr_swapped.reshape(ql, h * MLA_ROPE).astype(BF16),
  ], axis=1)
  return w_q, ovo


def _mla_proj_stages(rows, x_ref, mod_ref, g_ref, pos_ref, wlat_ref, qg_ref,
                     kvg_ref, wq_ref, freq_ref, sign_ref, q_ref, kv_ref):
  st = {}
  ql, kvl = MLA_Q_LORA, MLA_KV_LORA
  n = rows.stop - rows.start
  mid = rows.start + n // 2
  heads = q_ref.shape[0]
  hw = heads * LANES

  def project_latent():
    h = _norm_mod(x_ref[rows, :], g_ref[...], mod_ref[0:1, :],
                  mod_ref[1:2, :]).astype(BF16)
    st["lat"] = jnp.dot(h, wlat_ref[...], preferred_element_type=F32)

  def rotary_and_keys():
    lat = st["lat"]
    st["cq"] = (_rms(lat[:, :ql], RMS_EPS) * qg_ref[...]).astype(BF16)
    ckv = _rms(lat[:, ql:ql + kvl], RMS_EPS) * kvg_ref[...]
    low = lax.broadcasted_iota(jnp.int32, (1, LANES), 1) < LANES // 2
    pos = jnp.where(low, pos_ref[rows.start:mid, :], pos_ref[mid:rows.stop, :])
    ang = pos.astype(F32) * freq_ref[...]

    def spread(t):
      r = pltpu.roll(t, LANES // 2, axis=1)
      return jnp.concatenate([jnp.where(low, t, r), jnp.where(low, r, t)],
                             axis=0)

    cos = spread(jnp.cos(ang))
    sin = spread(jnp.sin(ang)) * sign_ref[...]
    kra = lat[:, ql + kvl:ql + kvl + LANES]
    krb = lat[:, ql + kvl + LANES:]
    kv_ref[rows, :LANES] = ckv.astype(BF16)
    kv_ref[rows, LANES:] = (kra * cos + krb * sin).astype(BF16)
    st["cos2"] = jnp.concatenate([cos, cos], axis=1)
    st["sin2"] = jnp.concatenate([sin, sin], axis=1)

  def queries(quads):
    cq = st["cq"]
    hr = heads * MLA_ROPE
    for u in quads:
      r0 = hw + 2 * u * LANES
      ra = jnp.dot(cq, wq_ref[:, r0:r0 + 2 * LANES],
                   preferred_element_type=F32)
      rb = jnp.dot(cq, wq_ref[:, hr + r0:hr + r0 + 2 * LANES],
                   preferred_element_type=F32)
      qr = ra * st["cos2"] + rb * st["sin2"]
      for v in range(2):
        c0 = (4 * u + 2 * v) * LANES
        qa = jnp.dot(cq, wq_ref[:, c0:c0 + 2 * LANES],
                     preferred_element_type=F32)
        pair = qr[:, v * LANES:(v + 1) * LANES]
        tiles = (pair, pltpu.roll(pair, LANES // 2, axis=1))
        for k in range(2):
          hd = 4 * u + 2 * v + k
          q_ref[hd, rows, :LANES] = qa[:, k * LANES:(k + 1) * LANES].astype(BF16)
          q_ref[hd, rows, LANES:] = tiles[k].astype(BF16)

  n_quads = heads // 4
  return [project_latent, rotary_and_keys,
          functools.partial(queries, range(0, n_quads // 2)),
          functools.partial(queries, range(n_quads // 2, n_quads))]


def _mla_proj_kernel(*refs, tm):
  half = tm // 2
  _staggered(_mla_proj_stages(slice(0, half), *refs),
             _mla_proj_stages(slice(half, tm), *refs))


def _mla_proj(x, mod, layer, g, positions, w_dq_dkv, q_norm_g, kv_norm_g, w_q):
  b, s, d = x.shape
  tm = PROJ_TOKEN_TILE
  ql, kvl, rope, h = MLA_Q_LORA, MLA_KV_LORA, MLA_ROPE, MLA_HEADS
  half = rope // 2
  assert 2 * rope == LANES and h % 4 == 0
  wkr = w_dq_dkv[:, ql + kvl:]
  zeros = jnp.zeros((d, LANES - rope), F32)
  w_lat = jnp.concatenate(
      [w_dq_dkv[:, :ql + kvl], wkr, zeros, wkr[:, half:], wkr[:, :half], zeros],
      axis=1).astype(BF16)
  inv_freq = ROPE_THETA ** (-jnp.arange(0, rope, 2, dtype=F32) / rope)
  freq = jnp.tile(inv_freq, LANES // half).reshape(1, LANES)
  sign = jnp.tile(jnp.concatenate([-jnp.ones((half,), F32),
                                   jnp.ones((half,), F32)]),
                  LANES // rope).reshape(1, LANES)
  return pl.pallas_call(
      functools.partial(_mla_proj_kernel, tm=tm),
      grid=(b, s // tm),
      in_specs=[
          pl.BlockSpec((None, tm, d), lambda bi, i: (bi, i, 0)),
          pl.BlockSpec((None, None, 6, d), lambda bi, i: (layer, bi, 0, 0)),
          _const_spec((1, d)),
          pl.BlockSpec((None, tm, 1), lambda bi, i: (bi, i, 0)),
          _const_spec(w_lat.shape),
          _const_spec((1, ql)),
          _const_spec((1, kvl)),
          _const_spec(w_q.shape),
          _const_spec((1, LANES)),
          _const_spec((1, LANES)),
      ],
      out_specs=[
          pl.BlockSpec((None, h, tm, QK_WIDTH), lambda bi, i: (bi, 0, i, 0)),
          pl.BlockSpec((None, tm, QK_WIDTH), lambda bi, i: (bi, i, 0)),
      ],
      out_shape=[
          jax.ShapeDtypeStruct((b, h, s, QK_WIDTH), BF16),
          jax.ShapeDtypeStruct((b, s, QK_WIDTH), BF16),
      ],
      compiler_params=_params("arbitrary", "arbitrary", fuse_inputs=10),
      name=f"mla_proj_{layer}",
  )(x, mod, g.reshape(1, d), positions.reshape(b, s, 1), w_lat,
    q_norm_g.reshape(1, ql), kv_norm_g.reshape(1, kvl), w_q, freq, sign)


def _flash_kernel(*refs, tq, n_cast):
  q_ref, kv_ref = refs[:2]
  cast_in = refs[2:2 + n_cast]
  o_ref = refs[2 + n_cast]
  cast_out = refs[3 + n_cast:3 + 2 * n_cast]
  m_scr, l_scr, acc_scr = refs[-3:]
  _WeightCast.run(cast_in, cast_out)
  i = pl.program_id(1)
  heads = q_ref.shape[0]

  def block(key_start, n_keys, row_lo=0, row_hi=tq, masked=False, first=False):
    n_rows = row_hi - row_lo
    kv = kv_ref[pl.ds(key_start, n_keys), :]
    v_ones = jnp.concatenate([kv[:, :LANES], jnp.ones((n_keys, LANES), BF16)],
                             axis=1)
    s, p, alpha = {}, {}, {}
    if masked:
      visible = (lax.broadcasted_iota(jnp.int32, (n_rows, n_keys), 1)
                 <= lax.broadcasted_iota(jnp.int32, (n_rows, n_keys), 0))

    def rows_of(hd):
      return slice(hd * tq + row_lo, hd * tq + row_hi)

    def scores(hd):
      s[hd] = lax.dot_general(q_ref[hd, row_lo:row_hi, :], kv,
                              (((1,), (1,)), ((), ())),
                              preferred_element_type=F32)

    def softmax(hd):
      rows = rows_of(hd)
      sg = s.pop(hd)
      if masked:
        sg = jnp.where(visible, sg, MASK_VALUE)
      m_cur = jnp.max(sg, axis=1, keepdims=True)
      if first:
        m_new = jnp.broadcast_to(m_cur, (n_rows, LANES))
      else:
        m_prev = m_scr[rows]
        m_new = jnp.maximum(m_prev, m_cur)
        alpha[hd] = jnp.exp2(m_prev - m_new)
      ps = [jnp.exp2(sg[:, k * LANES:(k + 1) * LANES] - m_new)
            for k in range(n_keys // LANES)]
      p[hd] = jnp.concatenate(ps, axis=1).astype(BF16)
      m_scr[rows] = m_new

    def weighted_values(hd):
      rows = rows_of(hd)
      pv = jnp.dot(p.pop(hd), v_ones, preferred_element_type=F32)
      if first:
        acc_scr[rows] = pv[:, :LANES]
        l_scr[rows] = pv[:, LANES:]
      else:
        a = alpha.pop(hd)
        acc_scr[rows] = a * acc_scr[rows] + pv[:, :LANES]
        l_scr[rows] = a * l_scr[rows] + pv[:, LANES:]

    scores(0)
    for hd in range(heads):
      if hd + 1 < heads:
        scores(hd + 1)
      softmax(hd)
      weighted_values(hd)

  diag = pl.multiple_of(i * tq, tq)
  half = tq // 2
  block(diag, half, masked=True, first=True)
  block(diag + half, half, row_lo=half, masked=True)
  wide = ATTN_WIDE_BLOCKS * tq

  def wide_block(j, carry):
    block(pl.multiple_of(j * wide, wide), wide)
    return carry

  n_wide = i // ATTN_WIDE_BLOCKS
  lax.fori_loop(0, n_wide, wide_block, 0)
  for r in range(ATTN_WIDE_BLOCKS - 1):
    @pl.when(i - n_wide * ATTN_WIDE_BLOCKS > r)
    def _(r=r):
      block(pl.multiple_of((n_wide * ATTN_WIDE_BLOCKS + r) * tq, tq), tq)
  for hd in range(heads):
    rows = slice(hd * tq, (hd + 1) * tq)
    o_ref[:, hd * LANES:(hd + 1) * LANES] = (
        acc_scr[rows] / l_scr[rows]).astype(BF16)


def _mla_attention(q, kv, side_casts=()):
  b, h, s, w = q.shape
  tq = ATTN_TILE
  rows = h * tq
  n_i = s // tq
  in_specs = [
      pl.BlockSpec((None, h, tq, w), lambda bi, i: (bi, 0, i, 0)),
      pl.BlockSpec((None, s, w), lambda bi, i: (bi, 0, 0)),
  ]
  args = [q, kv]
  out_specs = [pl.BlockSpec((None, tq, h * LANES), lambda bi, i: (bi, i, 0))]
  out_shape = [jax.ShapeDtypeStruct((b, s, h * LANES), BF16)]
  n_cast = 0
  if side_casts:
    cast = _WeightCast(side_casts, b * n_i, lambda bi, i: bi * n_i + i)
    in_specs += cast.in_specs
    args += cast.args
    out_specs += cast.out_specs
    out_shape += cast.out_shape
    n_cast = len(cast.args)
  outs = pl.pallas_call(
      functools.partial(_flash_kernel, tq=tq, n_cast=n_cast),
      grid=(b, n_i),
      in_specs=in_specs,
      out_specs=out_specs,
      out_shape=out_shape,
      scratch_shapes=[pltpu.VMEM((rows, LANES), F32)] * 3,
      compiler_params=_params("arbitrary", "arbitrary"),
      name="mla_attention",
  )(*args)
  return outs[0], tuple(outs[1:])


def _mlp_core(x, mod_ref, g_ref, w1_ref, w2_ref, fg_ref, h_scr, acc_scr, tf,
              side_work=()):
  h_scr[...] = _norm_mod(x, g_ref[...], mod_ref[3:4, :],
                         mod_ref[4:5, :]).astype(BF16)
  d_ff = w1_ref.shape[1]
  n_chunks = d_ff // tf
  side_work = list(side_work)
  for j in range(n_chunks):
    cols = slice(j * tf, (j + 1) * tf)
    a = jnp.dot(h_scr[...], w1_ref[:, cols], preferred_element_type=F32)
    a = jnp.maximum(a, 0.0)
    part = jnp.dot((a * a).astype(BF16), w2_ref[cols, :],
                   preferred_element_type=F32)
    if j == 0:
      acc_scr[...] = part
    else:
      acc_scr[...] += part
    slots = max(n_chunks - 1, 1)
    lo = min(-(-j * len(side_work) // slots), len(side_work))
    hi = min(-(-(j + 1) * len(side_work) // slots), len(side_work))
    for work in side_work[lo:hi]:
      work()
  out = x + mod_ref[5:6, :] * acc_scr[...]
  if fg_ref is not None:
    out = _rms(out, RMS_EPS) * fg_ref[...]
  return out


def _mlp_stages(rows, get_x, mod_ref, g_ref, w1_ref, w2_ref, fg_ref, o_ref,
                h_scr, acc_scr, tf):
  def normalise():
    x = get_x()
    o_ref[rows, :] = x
    h_scr[rows, :] = _norm_mod(x, g_ref[...], mod_ref[3:4, :],
                               mod_ref[4:5, :]).astype(BF16)

  def chunk(j):
    cols = slice(j * tf, (j + 1) * tf)
    a = jnp.dot(h_scr[rows, :], w1_ref[:, cols], preferred_element_type=F32)
    a = jnp.maximum(a, 0.0)
    part = jnp.dot((a * a).astype(BF16), w2_ref[cols, :],
                   preferred_element_type=F32)
    if j == 0:
      acc_scr[rows, :] = part
    else:
      acc_scr[rows, :] += part

  def finish():
    out = o_ref[rows, :] + mod_ref[5:6, :] * acc_scr[rows, :]
    if fg_ref is not None:
      out = _rms(out, RMS_EPS) * fg_ref[...]
    o_ref[rows, :] = out

  n_chunks = w1_ref.shape[1] // tf
  return ([normalise] + [functools.partial(chunk, j) for j in range(n_chunks)]
          + [finish])


def _mlp_kernel(*refs, tm, tf, with_attn, final):
  refs = list(refs)
  x_ref, mod_ref, g_ref, w1_ref, w2_ref = refs[:5]
  rest = refs[5:]
  if with_attn:
    attn_ref, wo_ref = rest[:2]
    rest = rest[2:]
  fg_ref = None
  if final:
    fg_ref = rest[0]
    rest = rest[1:]
  o_ref, h_scr, acc_scr = rest

  def stages(rows):
    def get_x():
      x = x_ref[rows, :]
      if with_attn:
        y = jnp.dot(attn_ref[rows, :], wo_ref[...], preferred_element_type=F32)
        x = x + mod_ref[2:3, :] * y
      return x
    return _mlp_stages(rows, get_x, mod_ref, g_ref, w1_ref, w2_ref, fg_ref,
                       o_ref, h_scr, acc_scr, tf)

  half = tm // 2
  _staggered(stages(slice(0, half)), stages(slice(half, tm)))


def _mlp_layer(x, mod, layer, g, w1, w2, attn=None, final_g=None):
  b, s, d = x.shape
  tm = MLP_TOKEN_TILE
  d_ff = w1.shape[-1]
  with_attn = attn is not None
  final = final_g is not None
  in_specs = [
      pl.BlockSpec((None, tm, d), lambda bi, i: (bi, i, 0)),
      pl.BlockSpec((None, None, 6, d), lambda bi, i: (layer, bi, 0, 0)),
      _const_spec((1, d)),
      _const_spec((d, d_ff)),
      _const_spec((d_ff, d)),
  ]
  args = [x, mod, g.reshape(1, d), w1, w2]
  if with_attn:
    attn_out, w_ovo = attn
    in_specs += [
        pl.BlockSpec((None, tm, attn_out.shape[-1]), lambda bi, i: (bi, i, 0)),
        _const_spec(w_ovo.shape),
    ]
    args += [attn_out, w_ovo]
  if final:
    in_specs.append(_const_spec((1, d)))
    args.append(final_g.reshape(1, d))
  return pl.pallas_call(
      functools.partial(_mlp_kernel, tm=tm, tf=MLP_FF_TILE,
                        with_attn=with_attn, final=final),
      grid=(b, s // tm),
      in_specs=in_specs,
      out_specs=pl.BlockSpec((None, tm, d), lambda bi, i: (bi, i, 0)),
      out_shape=jax.ShapeDtypeStruct(x.shape, F32),
      scratch_shapes=[pltpu.VMEM((tm, d), BF16), pltpu.VMEM((tm, d), F32)],
      compiler_params=_params("arbitrary", "arbitrary"),
      name=f"mlp_{layer}",
  )(*args)


def _pool_mlp_kernel(*refs, tm, tf, tiles_per_seq, final, n_cast):
  refs = list(refs)
  (x0_ref, xn_ref, xhn_ref, mod_ref, modn_ref, gmix_ref, pw_ref, ps_ref,
   g_ref, w1_ref, w2_ref) = refs[:11]
  rest = refs[11:]
  fg_ref = None
  if final:
    fg_ref = rest[0]
    rest = rest[1:]
  cast_in, rest = rest[:n_cast], rest[n_cast:]
  o_ref, rest = rest[0], rest[1:]
  cast_out, rest = rest[:n_cast], rest[n_cast:]
  x1_scr, a_scr, h_scr, acc_scr = rest
  _WeightCast.run(cast_in, cast_out)
  t = pl.program_id(0)
  slot = t % 2

  @pl.when(t == 0)
  def _():
    for stage in _pool_stages(x0_ref, lambda: x0_ref[0:POOL_HALO, :], 0,
                              gmix_ref, mod_ref, pw_ref, ps_ref, a_scr,
                              x1_scr.at[0], tm):
      stage()

  next_stages = _pool_stages(xn_ref, lambda: xhn_ref[...],
                             (t + 1) % tiles_per_seq, gmix_ref, modn_ref,
                             pw_ref, ps_ref, a_scr, x1_scr.at[1 - slot], tm)
  o_ref[...] = _mlp_core(x1_scr[slot], mod_ref, g_ref, w1_ref, w2_ref, fg_ref,
                         h_scr, acc_scr, tf, side_work=next_stages)


def _pool_mlp_layer(x, mod, layer, g_mix, pool_w, pool_scale, g, w1, w2,
                    final_g=None, side_casts=()):
  b, s, d = x.shape
  tm = TOKEN_TILE
  d_ff = w1.shape[-1]
  assert POOL_HALO >= max(POOL_WINDOWS) and tm % POOL_HALO == 0 and s % tm == 0
  tiles_per_seq = s // tm
  n_tiles = b * tiles_per_seq
  halo_per_tile = tm // POOL_HALO
  final = final_g is not None
  xf = x.reshape(b * s, d)

  def nxt(t):
    return jnp.minimum(t + 1, n_tiles - 1)

  in_specs = [
      pl.BlockSpec((tm, d), lambda t: (0, 0), pipeline_mode=pl.Buffered(1)),
      pl.BlockSpec((tm, d), lambda t: (nxt(t), 0)),
      pl.BlockSpec((POOL_HALO, d),
                   lambda t: (jnp.maximum(nxt(t) * halo_per_tile - 1, 0), 0)),
      pl.BlockSpec((None, None, 6, d),
                   lambda t: (layer, t // tiles_per_seq, 0, 0)),
      pl.BlockSpec((None, None, 6, d),
                   lambda t: (layer, nxt(t) // tiles_per_seq, 0, 0)),
      _const_spec((1, d)),
      _const_spec(pool_w.shape),
      _const_spec((1, d)),
      _const_spec((1, d)),
      _const_spec((d, d_ff)),
      _const_spec((d_ff, d)),
  ]
  args = [xf, xf, xf, mod, mod, g_mix.reshape(1, d), pool_w.astype(BF16),
          pool_scale.reshape(1, d), g.reshape(1, d), w1, w2]
  if final:
    in_specs.append(_const_spec((1, d)))
    args.append(final_g.reshape(1, d))
  out_specs = [pl.BlockSpec((tm, d), lambda t: (t, 0))]
  out_shape = [jax.ShapeDtypeStruct(xf.shape, F32)]
  n_cast = 0
  if side_casts:
    cast = _WeightCast(side_casts, n_tiles, lambda t: t)
    in_specs += cast.in_specs
    args += cast.args
    out_specs += cast.out_specs
    out_shape += cast.out_shape
    n_cast = len(cast.args)
  outs = pl.pallas_call(
      functools.partial(_pool_mlp_kernel, tm=tm, tf=MLP_FF_TILE,
                        tiles_per_seq=tiles_per_seq, final=final,
                        n_cast=n_cast),
      grid=(n_tiles,),
      in_specs=in_specs,
      out_specs=out_specs,
      out_shape=out_shape,
      scratch_shapes=[pltpu.VMEM((2, tm, d), F32),
                      pltpu.VMEM((POOL_HALO + tm, d), F32),
                      pltpu.VMEM((tm, d), BF16), pltpu.VMEM((tm, d), F32)],
      compiler_params=_params("arbitrary", fuse_inputs=len(args)),
      name=f"pool_mlp_{layer}",
  )(*args)
  return outs[0].reshape(b, s, d), tuple(outs[1:])


def kernel(x, c, positions, ada_w, ada_b, norm_mix_g, norm_mlp_g, pool_w, pool_scale, sgu_w_in, sgu_ln_g, sgu_ln_b, sgu_w_s, sgu_b_s, sgu_w_out, mla_w_dq_dkv, mla_q_norm_g, mla_kv_norm_g, mla_w_uq, mla_w_ukv, mla_w_o, mlp_w1, mlp_w2, final_g):
  depth = ada_w.shape[0]
  mod = _ada_mod(c, ada_w, ada_b)
  def big_weights(i):
    items = [(mlp_w1, i), (mlp_w2, i)]
    if i % N_MIXERS == 1:
      items += [(sgu_w_in, i // N_MIXERS), (sgu_w_out, i // N_MIXERS)]
    return items

  weights = [w[l].astype(BF16) for w, l in big_weights(0)]
  for i in range(depth):
    kind, j = i % N_MIXERS, i // N_MIXERS
    last = i == depth - 1
    fg = final_g if last else None
    side = () if last else big_weights(i + 1)
    w1, w2 = weights[:2]
    if kind == 0:
      x, weights = _pool_mlp_layer(
          x, mod, i, norm_mix_g[i], pool_w[j], pool_scale[j], norm_mlp_g[i],
          w1, w2, final_g=fg, side_casts=side)
    else:
      attn = None
      if kind == 1:
        x, weights = _sgu_layer(
            x, mod, i, norm_mix_g[i], weights[2], sgu_ln_g[j], sgu_ln_b[j],
            sgu_w_s[j], sgu_b_s[j], weights[3], side_casts=side)
      else:
        w_q, w_ovo = _mla_fold(mla_w_uq[j], mla_w_ukv[j], mla_w_o[j])
        q, kv = _mla_proj(x, mod, i, norm_mix_g[i], positions,
                          mla_w_dq_dkv[j], mla_q_norm_g[j], mla_kv_norm_g[j],
                          w_q)
        attn_out, weights = _mla_attention(q, kv, side_casts=side)
        attn = (attn_out, w_ovo)
      x = _mlp_layer(x, mod, i, norm_mlp_g[i], w1, w2, attn=attn, final_g=fg)
  return x
```

```python
import functools

import jax
import jax.numpy as jnp
from jax import lax
from jax.experimental import pallas as pl
from jax.experimental.pallas import tpu as pltpu

F32 = jnp.float32
BF16 = jnp.bfloat16

POOL_WINDOWS = (2, 4, 8, 16)
SGU_CHUNK = 128
SGU_HEAD_DIM = 128
MLA_HEADS = 16
MLA_Q_LORA = 256
MLA_KV_LORA = 128
MLA_NOPE = 128
MLA_ROPE = 64
MLA_V = 128
ROPE_THETA = 10000.0
RMS_EPS = 1e-6
LN_EPS = 1e-5
N_MIXERS = 3
MASK_VALUE = -1e30
LOG2_E = 1.4426950408889634
Q_SCALE = (MLA_NOPE + MLA_ROPE) ** -0.5 * LOG2_E

LANES = 128
VMEM_LIMIT_BYTES = 56 * 1024 * 1024

TOKEN_TILE = 512
MLP_TOKEN_TILE = 1024
SGU_TOKEN_TILE = 1024
PROJ_TOKEN_TILE = 1024
POOL_HALO = 16
ADA_COL_TILE = 2048
FOLD_HEADS_PER_STEP = 4
MLP_FF_TILE = 1024
ATTN_TILE = 512
ATTN_WIDE_BLOCKS = 3
QK_WIDTH = 2 * LANES


def _params(*semantics):
  return pltpu.CompilerParams(dimension_semantics=semantics,
                              vmem_limit_bytes=VMEM_LIMIT_BYTES)


def _const_spec(shape):
  zeros = (0,) * len(shape)
  return pl.BlockSpec(shape, lambda *_: zeros, pipeline_mode=pl.Buffered(1))


def _rms(x, eps):
  return x * lax.rsqrt(jnp.mean(x * x, axis=-1, keepdims=True) + eps)


def _norm_mod(x, g, shift, scale):
  return _rms(x, RMS_EPS) * g * (1.0 + scale) + shift


def _gelu(x):
  return 0.5 * x * (1.0 + lax.erf(x * (2.0 ** -0.5)))


def _staggered(stages_a, stages_b):
  n = len(stages_a)
  for k in range(n + 1):
    if k < n:
      stages_a[k]()
    if k >= 1:
      stages_b[k - 1]()


class _WeightCast:
  def __init__(self, items, n_steps, step_of):
    self.args = [w for w, _ in items]
    self.in_specs, self.out_specs, self.out_shape = [], [], []
    for w, layer in items:
      _, rows, cols = w.shape
      assert rows % n_steps == 0
      blk = rows // n_steps
      self.in_specs.append(pl.BlockSpec(
          (None, blk, cols),
          lambda *g, layer=layer: (layer, step_of(*g), 0)))
      self.out_specs.append(pl.BlockSpec(
          (blk, cols), lambda *g: (step_of(*g), 0)))
      self.out_shape.append(jax.ShapeDtypeStruct((rows, cols), BF16))

  @staticmethod
  def run(in_refs, out_refs):
    for src, dst in zip(in_refs, out_refs):
      dst[...] = src[...].astype(BF16)


def _ada_kernel(c_ref, w_ref, b_ref, o_ref):
  c = c_ref[...]
  ca = (c * jax.nn.sigmoid(c)).astype(BF16)
  o_ref[...] = jnp.dot(ca, w_ref[...].astype(BF16),
                       preferred_element_type=F32) + b_ref[...]


def _ada_mod(c, ada_w, ada_b):
  depth, d, d6 = ada_w.shape
  b = c.shape[0]
  tn = ADA_COL_TILE
  out = pl.pallas_call(
      _ada_kernel,
      grid=(depth, d6 // tn),
      in_specs=[
          pl.BlockSpec((b, d), lambda l, j: (0, 0)),
          pl.BlockSpec((None, d, tn), lambda l, j: (l, 0, j)),
          pl.BlockSpec((None, 1, tn), lambda l, j: (l, 0, j)),
      ],
      out_specs=pl.BlockSpec((None, b, tn), lambda l, j: (l, 0, j)),
      out_shape=jax.ShapeDtypeStruct((depth, b, d6), F32),
      compiler_params=_params("arbitrary", "arbitrary"),
      name="ada_mod",
  )(c, ada_w, ada_b.reshape(depth, 1, d6))
  return out.reshape(depth, b, 6, d)


def _pool_stages(x_ref, xh, i, g_ref, mod_ref, w_ref, ps_ref, a_scr, out_ref,
                 tm):
  def prep():
    g, shift, scale = g_ref[...], mod_ref[0:1, :], mod_ref[1:2, :]
    hh = _norm_mod(xh(), g, shift, scale)
    a_scr[0:POOL_HALO, :] = jnp.where(i > 0, hh, 0.0)
    a_scr[POOL_HALO:, :] = _norm_mod(x_ref[...], g, shift, scale)

  def group(gi, win):
    gdim = w_ref.shape[1]
    cols = slice(gi * gdim, (gi + 1) * gdim)
    s = a_scr[:, cols]
    k = 1
    while k < win:
      s = s + pltpu.roll(s, k, axis=0)
      k *= 2
    t = (i * tm + lax.broadcasted_iota(jnp.int32, (tm, 1), 0)).astype(F32)
    inv_count = 1.0 / jnp.minimum(t + 1.0, float(win))
    pooled = s[POOL_HALO:] * inv_count - a_scr[POOL_HALO:, cols]
    y = jnp.dot(pooled.astype(BF16), w_ref[gi], preferred_element_type=F32)
    out_ref[:, cols] = x_ref[:, cols] + mod_ref[2:3, cols] * (
        y * ps_ref[:, cols])

  return [prep] + [functools.partial(group, gi, win)
                   for gi, win in enumerate(POOL_WINDOWS)]


def _sgu_stages(rows, n_rows, x_ref, mod_ref, g_ref, win_ref, lng_ref, lnb_ref,
                ws_ref, bst_ref, wout_ref, o_ref, gated_scr):
  st = {}
  width = wout_ref.shape[0]
  t = SGU_CHUNK

  def project_v():
    st["h"] = _norm_mod(x_ref[rows, :], g_ref[...], mod_ref[0:1, :],
                        mod_ref[1:2, :]).astype(BF16)
    st["v"] = jnp.dot(st["h"], win_ref[:, width:], preferred_element_type=F32)

  def project_u():
    st["u"] = _gelu(jnp.dot(st["h"], win_ref[:, :width],
                            preferred_element_type=F32))

  def normalise():
    v = _gelu(st["v"])
    mu = jnp.mean(v, axis=-1, keepdims=True)
    vc = v - mu
    var = jnp.mean(vc * vc, axis=-1, keepdims=True)
    st["v"] = (vc * lax.rsqrt(var + LN_EPS) * lng_ref[...]
               + lnb_ref[...]).astype(BF16)

  def gate_spatial():
    u, v = st["u"], st["v"]
    causal = (lax.broadcasted_iota(jnp.int32, (t, t), 1)
              <= lax.broadcasted_iota(jnp.int32, (t, t), 0))
    for hd in range(ws_ref.shape[0]):
      cols = slice(hd * SGU_HEAD_DIM, (hd + 1) * SGU_HEAD_DIM)
      ws = jnp.where(causal, ws_ref[hd], 0.0).astype(BF16)
      bias = bst_ref[:, hd:hd + 1]
      for c in range(n_rows // t):
        blk = slice(c * t, (c + 1) * t)
        mixed = jnp.dot(ws, v[blk, cols], preferred_element_type=F32) + bias
        gated_scr[rows.start + c * t:rows.start + (c + 1) * t, cols] = (
            u[blk, cols] * mixed).astype(BF16)

  def project_out():
    y = jnp.dot(gated_scr[rows, :], wout_ref[...], preferred_element_type=F32)
    o_ref[rows, :] = x_ref[rows, :] + mod_ref[2:3, :] * y

  return [project_v, project_u, normalise, gate_spatial, project_out]


def _sgu_kernel(*refs, tm, n_cast):
  n_in = 9
  ins, cast_in = refs[:n_in], refs[n_in:n_in + n_cast]
  o_ref = refs[n_in + n_cast]
  cast_out = refs[n_in + n_cast + 1:n_in + 2 * n_cast + 1]
  gated_scr = refs[-1]
  _WeightCast.run(cast_in, cast_out)
  half = tm // 2
  _staggered(_sgu_stages(slice(0, half), half, *ins, o_ref, gated_scr),
             _sgu_stages(slice(half, tm), half, *ins, o_ref, gated_scr))


def _sgu_layer(x, mod, layer, g, w_in, ln_g, ln_b, w_s, b_s, w_out,
               side_casts=()):
  b, s, d = x.shape
  tm = SGU_TOKEN_TILE
  width = w_out.shape[0]
  heads, t, _ = w_s.shape
  assert t == SGU_CHUNK and tm % t == 0 and heads * SGU_HEAD_DIM == width
  n_i = s // tm
  in_specs = [
      pl.BlockSpec((None, tm, d), lambda bi, i: (bi, i, 0)),
      pl.BlockSpec((None, None, 6, d), lambda bi, i: (layer, bi, 0, 0)),
      _const_spec((1, d)),
      _const_spec((d, 2 * width)),
      _const_spec((1, width)),
      _const_spec((1, width)),
      _const_spec((heads, t, t)),
      _const_spec((t, heads)),
      _const_spec((width, d)),
  ]
  args = [x, mod, g.reshape(1, d), w_in, ln_g.reshape(1, width),
          ln_b.reshape(1, width), w_s, b_s.T, w_out]
  out_specs = [pl.BlockSpec((None, tm, d), lambda bi, i: (bi, i, 0))]
  out_shape = [jax.ShapeDtypeStruct(x.shape, F32)]
  n_cast = 0
  if side_casts:
    cast = _WeightCast(side_casts, b * n_i, lambda bi, i: bi * n_i + i)
    in_specs += cast.in_specs
    args += cast.args
    out_specs += cast.out_specs
    out_shape += cast.out_shape
    n_cast = len(cast.args)
  outs = pl.pallas_call(
      functools.partial(_sgu_kernel, tm=tm, n_cast=n_cast),
      grid=(b, n_i),
      in_specs=in_specs,
      out_specs=out_specs,
      out_shape=out_shape,
      scratch_shapes=[pltpu.VMEM((tm, width), BF16)],
      compiler_params=_params("arbitrary", "arbitrary"),
      name=f"sgu_mixer_{layer}",
  )(*args)
  return outs[0], tuple(outs[1:])


def _mla_fold_kernel(wqn_ref, wuk_ref, wuv_ref, wo_ref, qabs_ref, ovo_ref):
  kvl = wuv_ref.shape[1]
  for k in range(wqn_ref.shape[0]):
    qabs = lax.dot_general(
        wqn_ref[k], wuk_ref[k], (((1,), (1,)), ((), ())),
        precision=lax.Precision.HIGHEST, preferred_element_type=F32)
    qabs_ref[k] = (qabs * Q_SCALE).astype(BF16)
    ovo_ref[k * kvl:(k + 1) * kvl, :] = jnp.dot(
        wuv_ref[k], wo_ref[k], precision=lax.Precision.HIGHEST,
        preferred_element_type=F32).astype(BF16)


def _mla_fold(w_uq, w_ukv, w_o):
  h, ql, kvl = MLA_HEADS, MLA_Q_LORA, MLA_KV_LORA
  d = w_o.shape[1]
  w_uq3 = w_uq.reshape(ql, h, MLA_NOPE + MLA_ROPE)
  w_ukv3 = w_ukv.reshape(kvl, h, MLA_NOPE + MLA_V)
  wqn = w_uq3[:, :, :MLA_NOPE].transpose(1, 0, 2)
  wuk = w_ukv3[:, :, :MLA_NOPE].transpose(1, 0, 2)
  wuv = w_ukv3[:, :, MLA_NOPE:].transpose(1, 0, 2)
  wo = w_o.reshape(h, MLA_V, d)
  hb = FOLD_HEADS_PER_STEP
  assert h % hb == 0
  qabs, ovo = pl.pallas_call(
      _mla_fold_kernel,
      grid=(h // hb,),
      in_specs=[
          pl.BlockSpec((hb, ql, MLA_NOPE), lambda i: (i, 0, 0)),
          pl.BlockSpec((hb, kvl, MLA_NOPE), lambda i: (i, 0, 0)),
          pl.BlockSpec((hb, kvl, MLA_V), lambda i: (i, 0, 0)),
          pl.BlockSpec((hb, MLA_V, d), lambda i: (i, 0, 0)),
      ],
      out_specs=[
          pl.BlockSpec((hb, ql, kvl), lambda i: (i, 0, 0)),
          pl.BlockSpec((hb * kvl, d), lambda i: (i, 0)),
      ],
      out_shape=[
          jax.ShapeDtypeStruct((h, ql, kvl), BF16),
          jax.ShapeDtypeStruct((h * kvl, d), BF16),
      ],
      compiler_params=_params("arbitrary"),
      name="mla_fold",
  )(wqn, wuk, wuv, wo)
  half = MLA_ROPE // 2
  wr = w_uq3[:, :, MLA_NOPE:] * Q_SCALE
  wr_swapped = jnp.concatenate([wr[:, :, half:], wr[:, :, :half]], axis=-1)
  w_q = jnp.concatenate([
      qabs.transpose(1, 0, 2).reshape(ql, h * kvl),
      wr.reshape(ql, h * MLA_ROPE).astype(BF16),
      wr_swapped.reshape(ql, h * MLA_ROPE).astype(BF16),
  ], axis=1)
  return w_q, ovo


def _mla_proj_stages(rows, x_ref, mod_ref, g_ref, pos_ref, wlat_ref, qg_ref,
                     kvg_ref, wq_ref, freq_ref, sign_ref, q_ref, kv_ref):
  st = {}
  ql, kvl = MLA_Q_LORA, MLA_KV_LORA
  n = rows.stop - rows.start
  mid = rows.start + n // 2
  heads = q_ref.shape[0]
  hw = heads * LANES

  def project_latent():
    h = _norm_mod(x_ref[rows, :], g_ref[...], mod_ref[0:1, :],
                  mod_ref[1:2, :]).astype(BF16)
    st["lat"] = jnp.dot(h, wlat_ref[...], preferred_element_type=F32)

  def rotary_and_keys():
    lat = st["lat"]
    st["cq"] = (_rms(lat[:, :ql], RMS_EPS) * qg_ref[...]).astype(BF16)
    ckv = _rms(lat[:, ql:ql + kvl], RMS_EPS) * kvg_ref[...]
    low = lax.broadcasted_iota(jnp.int32, (1, LANES), 1) < LANES // 2
    pos = jnp.where(low, pos_ref[rows.start:mid, :], pos_ref[mid:rows.stop, :])
    ang = pos.astype(F32) * freq_ref[...]

    def spread(t):
      r = pltpu.roll(t, LANES // 2, axis=1)
      return jnp.concatenate([jnp.where(low, t, r), jnp.where(low, r, t)],
                             axis=0)

    cos = spread(jnp.cos(ang))
    sin = spread(jnp.sin(ang)) * sign_ref[...]
    kra = lat[:, ql + kvl:ql + kvl + LANES]
    krb = lat[:, ql + kvl + LANES:]
    kv_ref[rows, :LANES] = ckv.astype(BF16)
    kv_ref[rows, LANES:] = (kra * cos + krb * sin).astype(BF16)
    st["cos2"] = jnp.concatenate([cos, cos], axis=1)
    st["sin2"] = jnp.concatenate([sin, sin], axis=1)

  def queries(quads):
    cq = st["cq"]
    hr = heads * MLA_ROPE
    for u in quads:
      r0 = hw + 2 * u * LANES
      ra = jnp.dot(cq, wq_ref[:, r0:r0 + 2 * LANES],
                   preferred_element_type=F32)
      rb = jnp.dot(cq, wq_ref[:, hr + r0:hr + r0 + 2 * LANES],
                   preferred_element_type=F32)
      qr = ra * st["cos2"] + rb * st["sin2"]
      for v in range(2):
        c0 = (4 * u + 2 * v) * LANES
        qa = jnp.dot(cq, wq_ref[:, c0:c0 + 2 * LANES],
                     preferred_element_type=F32)
        pair = qr[:, v * LANES:(v + 1) * LANES]
        tiles = (pair, pltpu.roll(pair, LANES // 2, axis=1))
        for k in range(2):
          hd = 4 * u + 2 * v + k
          q_ref[hd, rows, :LANES] = qa[:, k * LANES:(k + 1) * LANES].astype(BF16)
          q_ref[hd, rows, LANES:] = tiles[k].astype(BF16)

  n_quads = heads // 4
  return [project_latent, rotary_and_keys,
          functools.partial(queries, range(0, n_quads // 2)),
          functools.partial(queries, range(n_quads // 2, n_quads))]


def _mla_proj_kernel(*refs, tm):
  half = tm // 2
  _staggered(_mla_proj_stages(slice(0, half), *refs),
             _mla_proj_stages(slice(half, tm), *refs))


def _mla_proj(x, mod, layer, g, positions, w_dq_dkv, q_norm_g, kv_norm_g, w_q):
  b, s, d = x.shape
  tm = PROJ_TOKEN_TILE
  ql, kvl, rope, h = MLA_Q_LORA, MLA_KV_LORA, MLA_ROPE, MLA_HEADS
  half = rope // 2
  assert 2 * rope == LANES and h % 4 == 0
  wkr = w_dq_dkv[:, ql + kvl:]
  zeros = jnp.zeros((d, LANES - rope), F32)
  w_lat = jnp.concatenate(
      [w_dq_dkv[:, :ql + kvl], wkr, zeros, wkr[:, half:], wkr[:, :half], zeros],
      axis=1).astype(BF16)
  inv_freq = ROPE_THETA ** (-jnp.arange(0, rope, 2, dtype=F32) / rope)
  freq = jnp.tile(inv_freq, LANES // half).reshape(1, LANES)
  sign = jnp.tile(jnp.concatenate([-jnp.ones((half,), F32),
                                   jnp.ones((half,), F32)]),
                  LANES // rope).reshape(1, LANES)
  return pl.pallas_call(
      functools.partial(_mla_proj_kernel, tm=tm),
      grid=(b, s // tm),
      in_specs=[
          pl.BlockSpec((None, tm, d), lambda bi, i: (bi, i, 0)),
          pl.BlockSpec((None, None, 6, d), lambda bi, i: (layer, bi, 0, 0)),
          _const_spec((1, d)),
          pl.BlockSpec((None, tm, 1), lambda bi, i: (bi, i, 0)),
          _const_spec(w_lat.shape),
          _const_spec((1, ql)),
          _const_spec((1, kvl)),
          _const_spec(w_q.shape),
          _const_spec((1, LANES)),
          _const_spec((1, LANES)),
      ],
      out_specs=[
          pl.BlockSpec((None, h, tm, QK_WIDTH), lambda bi, i: (bi, 0, i, 0)),
          pl.BlockSpec((None, tm, QK_WIDTH), lambda bi, i: (bi, i, 0)),
      ],
      out_shape=[
          jax.ShapeDtypeStruct((b, h, s, QK_WIDTH), BF16),
          jax.ShapeDtypeStruct((b, s, QK_WIDTH), BF16),
      ],
      compiler_params=_params("arbitrary", "arbitrary"),
      name=f"mla_proj_{layer}",
  )(x, mod, g.reshape(1, d), positions.reshape(b, s, 1), w_lat,
    q_norm_g.reshape(1, ql), kv_norm_g.reshape(1, kvl), w_q, freq, sign)


def _flash_kernel(*refs, tq, n_cast):
  q_ref, kv_ref = refs[:2]
  cast_in = refs[2:2 + n_cast]
  o_ref = refs[2 + n_cast]
  cast_out = refs[3 + n_cast:3 + 2 * n_cast]
  m_scr, l_scr, acc_scr = refs[-3:]
  _WeightCast.run(cast_in, cast_out)
  i = pl.program_id(1)
  heads = q_ref.shape[0]

  def block(key_start, n_keys, row_lo=0, row_hi=tq, masked=False, first=False):
    n_rows = row_hi - row_lo
    kv = kv_ref[pl.ds(key_start, n_keys), :]
    v_ones = jnp.concatenate([kv[:, :LANES], jnp.ones((n_keys, LANES), BF16)],
                             axis=1)
    s, p, alpha = {}, {}, {}
    if masked:
      visible = (lax.broadcasted_iota(jnp.int32, (n_rows, n_keys), 1)
                 <= lax.broadcasted_iota(jnp.int32, (n_rows, n_keys), 0))

    def rows_of(hd):
      return slice(hd * tq + row_lo, hd * tq + row_hi)

    def scores(hd):
      s[hd] = lax.dot_general(q_ref[hd, row_lo:row_hi, :], kv,
                              (((1,), (1,)), ((), ())),
                              preferred_element_type=F32)

    def softmax(hd):
      rows = rows_of(hd)
      sg = s.pop(hd)
      if masked:
        sg = jnp.where(visible, sg, MASK_VALUE)
      m_cur = jnp.max(sg, axis=1, keepdims=True)
      if first:
        m_new = jnp.broadcast_to(m_cur, (n_rows, LANES))
      else:
        m_prev = m_scr[rows]
        m_new = jnp.maximum(m_prev, m_cur)
        alpha[hd] = jnp.exp2(m_prev - m_new)
      ps = [jnp.exp2(sg[:, k * LANES:(k + 1) * LANES] - m_new)
            for k in range(n_keys // LANES)]
      p[hd] = jnp.concatenate(ps, axis=1).astype(BF16)
      m_scr[rows] = m_new

    def weighted_values(hd):
      rows = rows_of(hd)
      pv = jnp.dot(p.pop(hd), v_ones, preferred_element_type=F32)
      if first:
        acc_scr[rows] = pv[:, :LANES]
        l_scr[rows] = pv[:, LANES:]
      else:
        a = alpha.pop(hd)
        acc_scr[rows] = a * acc_scr[rows] + pv[:, :LANES]
        l_scr[rows] = a * l_scr[rows] + pv[:, LANES:]

    scores(0)
    for hd in range(heads):
      if hd + 1 < heads:
        scores(hd + 1)
      softmax(hd)
      weighted_values(hd)

  diag = pl.multiple_of(i * tq, tq)
  half = tq // 2
  block(diag, half, masked=True, first=True)
  block(diag + half, half, row_lo=half, masked=True)
  wide = ATTN_WIDE_BLOCKS * tq

  def wide_block(j, carry):
    block(pl.multiple_of(j * wide, wide), wide)
    return carry

  n_wide = i // ATTN_WIDE_BLOCKS
  lax.fori_loop(0, n_wide, wide_block, 0)
  for r in range(ATTN_WIDE_BLOCKS - 1):
    @pl.when(i - n_wide * ATTN_WIDE_BLOCKS > r)
    def _(r=r):
      block(pl.multiple_of((n_wide * ATTN_WIDE_BLOCKS + r) * tq, tq), tq)
  for hd in range(heads):
    rows = slice(hd * tq, (hd + 1) * tq)
    o_ref[:, hd * LANES:(hd + 1) * LANES] = (
        acc_scr[rows] / l_scr[rows]).astype(BF16)


def _mla_attention(q, kv, side_casts=()):
  b, h, s, w = q.shape
  tq = ATTN_TILE
  rows = h * tq
  n_i = s // tq
  in_specs = [
      pl.BlockSpec((None, h, tq, w), lambda bi, i: (bi, 0, i, 0)),
      pl.BlockSpec((None, s, w), lambda bi, i: (bi, 0, 0)),
  ]
  args = [q, kv]
  out_specs = [pl.BlockSpec((None, tq, h * LANES), lambda bi, i: (bi, i, 0))]
  out_shape = [jax.ShapeDtypeStruct((b, s, h * LANES), BF16)]
  n_cast = 0
  if side_casts:
    cast = _WeightCast(side_casts, b * n_i, lambda bi, i: bi * n_i + i)
    in_specs += cast.in_specs
    args += cast.args
    out_specs += cast.out_specs
    out_shape += cast.out_shape
    n_cast = len(cast.args)
  outs = pl.pallas_call(
      functools.partial(_flash_kernel, tq=tq, n_cast=n_cast),
      grid=(b, n_i),
      in_specs=in_specs,
      out_specs=out_specs,
      out_shape=out_shape,
      scratch_shapes=[pltpu.VMEM((rows, LANES), F32)] * 3,
      compiler_params=_params("arbitrary", "arbitrary"),
      name="mla_attention",
  )(*args)
  return outs[0], tuple(outs[1:])


def _mlp_core(x, mod_ref, g_ref, w1_ref, w2_ref, fg_ref, h_scr, acc_scr, tf,
              side_work=()):
  h_scr[...] = _norm_mod(x, g_ref[...], mod_ref[3:4, :],
                         mod_ref[4:5, :]).astype(BF16)
  d_ff = w1_ref.shape[1]
  n_chunks = d_ff // tf
  side_work = list(side_work)
  for j in range(n_chunks):
    cols = slice(j * tf, (j + 1) * tf)
    a = jnp.dot(h_scr[...], w1_ref[:, cols], preferred_element_type=F32)
    a = jnp.maximum(a, 0.0)
    part = jnp.dot((a * a).astype(BF16), w2_ref[cols, :],
                   preferred_element_type=F32)
    if j == 0:
      acc_scr[...] = part
    else:
      acc_scr[...] += part
    slots = max(n_chunks - 1, 1)
    lo = min(-(-j * len(side_work) // slots), len(side_work))
    hi = min(-(-(j + 1) * len(side_work) // slots), len(side_work))
    for work in side_work[lo:hi]:
      work()
  out = x + mod_ref[5:6, :] * acc_scr[...]
  if fg_ref is not None:
    out = _rms(out, RMS_EPS) * fg_ref[...]
  return out


def _mlp_stages(rows, get_x, mod_ref, g_ref, w1_ref, w2_ref, fg_ref, o_ref,
                h_scr, acc_scr, tf):
  def normalise():
    x = get_x()
    o_ref[rows, :] = x
    h_scr[rows, :] = _norm_mod(x, g_ref[...], mod_ref[3:4, :],
                               mod_ref[4:5, :]).astype(BF16)

  def chunk(j):
    cols = slice(j * tf, (j + 1) * tf)
    a = jnp.dot(h_scr[rows, :], w1_ref[:, cols], preferred_element_type=F32)
    a = jnp.maximum(a, 0.0)
    part = jnp.dot((a * a).astype(BF16), w2_ref[cols, :],
                   preferred_element_type=F32)
    if j == 0:
      acc_scr[rows, :] = part
    else:
      acc_scr[rows, :] += part

  def finish():
    out = o_ref[rows, :] + mod_ref[5:6, :] * acc_scr[rows, :]
    if fg_ref is not None:
      out = _rms(out, RMS_EPS) * fg_ref[...]
    o_ref[rows, :] = out

  n_chunks = w1_ref.shape[1] // tf
  return ([normalise] + [functools.partial(chunk, j) for j in range(n_chunks)]
          + [finish])


def _mlp_kernel(*refs, tm, tf, with_attn, final):
  refs = list(refs)
  x_ref, mod_ref, g_ref, w1_ref, w2_ref = refs[:5]
  rest = refs[5:]
  if with_attn:
    attn_ref, wo_ref = rest[:2]
    rest = rest[2:]
  fg_ref = None
  if final:
    fg_ref = rest[0]
    rest = rest[1:]
  o_ref, h_scr, acc_scr = rest

  def stages(rows):
    def get_x():
      x = x_ref[rows, :]
      if with_attn:
        y = jnp.dot(attn_ref[rows, :], wo_ref[...], preferred_element_type=F32)
        x = x + mod_ref[2:3, :] * y
      return x
    return _mlp_stages(rows, get_x, mod_ref, g_ref, w1_ref, w2_ref, fg_ref,
                       o_ref, h_scr, acc_scr, tf)

  half = tm // 2
  _staggered(stages(slice(0, half)), stages(slice(half, tm)))


def _mlp_layer(x, mod, layer, g, w1, w2, attn=None, final_g=None):
  b, s, d = x.shape
  tm = MLP_TOKEN_TILE
  d_ff = w1.shape[-1]
  with_attn = attn is not None
  final = final_g is not None
  in_specs = [
      pl.BlockSpec((None, tm, d), lambda bi, i: (bi, i, 0)),
      pl.BlockSpec((None, None, 6, d), lambda bi, i: (layer, bi, 0, 0)),
      _const_spec((1, d)),
      _const_spec((d, d_ff)),
      _const_spec((d_ff, d)),
  ]
  args = [x, mod, g.reshape(1, d), w1, w2]
  if with_attn:
    attn_out, w_ovo = attn
    in_specs += [
        pl.BlockSpec((None, tm, attn_out.shape[-1]), lambda bi, i: (bi, i, 0)),
        _const_spec(w_ovo.shape),
    ]
    args += [attn_out, w_ovo]
  if final:
    in_specs.append(_const_spec((1, d)))
    args.append(final_g.reshape(1, d))
  return pl.pallas_call(
      functools.partial(_mlp_kernel, tm=tm, tf=MLP_FF_TILE,
                        with_attn=with_attn, final=final),
      grid=(b, s // tm),
      in_specs=in_specs,
      out_specs=pl.BlockSpec((None, tm, d), lambda bi, i: (bi, i, 0)),
      out_shape=jax.ShapeDtypeStruct(x.shape, F32),
      scratch_shapes=[pltpu.VMEM((tm, d), BF16), pltpu.VMEM((tm, d), F32)],
      compiler_params=_params("arbitrary", "arbitrary"),
      name=f"mlp_{layer}",
  )(*args)


def _pool_mlp_kernel(*refs, tm, tf, tiles_per_seq, final, n_cast):
  refs = list(refs)
  (x0_ref, xn_ref, xhn_ref, mod_ref, modn_ref, gmix_ref, pw_ref, ps_ref,
   g_ref, w1_ref, w2_ref) = refs[:11]
  rest = refs[11:]
  fg_ref = None
  if final:
    fg_ref = rest[0]
    rest = rest[1:]
  cast_in, rest = rest[:n_cast], rest[n_cast:]
  o_ref, rest = rest[0], rest[1:]
  cast_out, rest = rest[:n_cast], rest[n_cast:]
  x1_scr, a_scr, h_scr, acc_scr = rest
  _WeightCast.run(cast_in, cast_out)
  t = pl.program_id(0)
  slot = t % 2

  @pl.when(t == 0)
  def _():
    for stage in _pool_stages(x0_ref, lambda: x0_ref[0:POOL_HALO, :], 0,
                              gmix_ref, mod_ref, pw_ref, ps_ref, a_scr,
                              x1_scr.at[0], tm):
      stage()

  next_stages = _pool_stages(xn_ref, lambda: xhn_ref[...],
                             (t + 1) % tiles_per_seq, gmix_ref, modn_ref,
                             pw_ref, ps_ref, a_scr, x1_scr.at[1 - slot], tm)
  o_ref[...] = _mlp_core(x1_scr[slot], mod_ref, g_ref, w1_ref, w2_ref, fg_ref,
                         h_scr, acc_scr, tf, side_work=next_stages)


def _pool_mlp_layer(x, mod, layer, g_mix, pool_w, pool_scale, g, w1, w2,
                    final_g=None, side_casts=()):
  b, s, d = x.shape
  tm = TOKEN_TILE
  d_ff = w1.shape[-1]
  assert POOL_HALO >= max(POOL_WINDOWS) and tm % POOL_HALO == 0 and s % tm == 0
  tiles_per_seq = s // tm
  n_tiles = b * tiles_per_seq
  halo_per_tile = tm // POOL_HALO
  final = final_g is not None
  xf = x.reshape(b * s, d)

  def nxt(t):
    return jnp.minimum(t + 1, n_tiles - 1)

  in_specs = [
      pl.BlockSpec((tm, d), lambda t: (0, 0), pipeline_mode=pl.Buffered(1)),
      pl.BlockSpec((tm, d), lambda t: (nxt(t), 0)),
      pl.BlockSpec((POOL_HALO, d),
                   lambda t: (jnp.maximum(nxt(t) * halo_per_tile - 1, 0), 0)),
      pl.BlockSpec((None, None, 6, d),
                   lambda t: (layer, t // tiles_per_seq, 0, 0)),
      pl.BlockSpec((None, None, 6, d),
                   lambda t: (layer, nxt(t) // tiles_per_seq, 0, 0)),
      _const_spec((1, d)),
      _const_spec(pool_w.shape),
      _const_spec((1, d)),
      _const_spec((1, d)),
      _const_spec((d, d_ff)),
      _const_spec((d_ff, d)),
  ]
  args = [xf, xf, xf, mod, mod, g_mix.reshape(1, d), pool_w.astype(BF16),
          pool_scale.reshape(1, d), g.reshape(1, d), w1, w2]
  if final:
    in_specs.append(_const_spec((1, d)))
    args.append(final_g.reshape(1, d))
  out_specs = [pl.BlockSpec((tm, d), lambda t: (t, 0))]
  out_shape = [jax.ShapeDtypeStruct(xf.shape, F32)]
  n_cast = 0
  if side_casts:
    cast = _WeightCast(side_casts, n_tiles, lambda t: t)
    in_specs += cast.in_specs
    args += cast.args
    out_specs += cast.out_specs
    out_shape += cast.out_shape
    n_cast = len(cast.args)
  outs = pl.pallas_call(
      functools.partial(_pool_mlp_kernel, tm=tm, tf=MLP_FF_TILE,
                        tiles_per_seq=tiles_per_seq, final=final,
                        n_cast=n_cast),
      grid=(n_tiles,),
      in_specs=in_specs,
      out_specs=out_specs,
      out_shape=out_shape,
      scratch_shapes=[pltpu.VMEM((2, tm, d), F32),
                      pltpu.VMEM((POOL_HALO + tm, d), F32),
                      pltpu.VMEM((tm, d), BF16), pltpu.VMEM((tm, d), F32)],
      compiler_params=_params("arbitrary"),
      name=f"pool_mlp_{layer}",
  )(*args)
  return outs[0].reshape(b, s, d), tuple(outs[1:])


def kernel(x, c, positions, ada_w, ada_b, norm_mix_g, norm_mlp_g, pool_w, pool_scale, sgu_w_in, sgu_ln_g, sgu_ln_b, sgu_w_s, sgu_b_s, sgu_w_out, mla_w_dq_dkv, mla_q_norm_g, mla_kv_norm_g, mla_w_uq, mla_w_ukv, mla_w_o, mlp_w1, mlp_w2, final_g):
  depth = ada_w.shape[0]
  mod = _ada_mod(c, ada_w, ada_b)
  def big_weights(i):
    items = [(mlp_w1, i), (mlp_w2, i)]
    if i % N_MIXERS == 1:
      items += [(sgu_w_in, i // N_MIXERS), (sgu_w_out, i // N_MIXERS)]
    return items

  weights = [w[l].astype(BF16) for w, l in big_weights(0)]
  for i in range(depth):
    kind, j = i % N_MIXERS, i // N_MIXERS
    last = i == depth - 1
    fg = final_g if last else None
    side = () if last else big_weights(i + 1)
    w1, w2 = weights[:2]
    if kind == 0:
      x, weights = _pool_mlp_layer(
          x, mod, i, norm_mix_g[i], pool_w[j], pool_scale[j], norm_mlp_g[i],
          w1, w2, final_g=fg, side_casts=side)
    else:
      attn = None
      if kind == 1:
        x, weights = _sgu_layer(
            x, mod, i, norm_mix_g[i], weights[2], sgu_ln_g[j], sgu_ln_b[j],
            sgu_w_s[j], sgu_b_s[j], weights[3], side_casts=side)
      else:
        w_q, w_ovo = _mla_fold(mla_w_uq[j], mla_w_ukv[j], mla_w_o[j])
        q, kv = _mla_proj(x, mod, i, norm_mix_g[i], positions,
                          mla_w_dq_dkv[j], mla_q_norm_g[j], mla_kv_norm_g[j],
                          w_q)
        attn_out, weights = _mla_attention(q, kv, side_casts=side)
        attn = (attn_out, w_ovo)
      x = _mlp_layer(x, mod, i, norm_mlp_g[i], w1, w2, attn=attn, final_g=fg)
  return x
```
